```python
import jax, jax.numpy as jnp
from jax import lax
import numpy as np

D_MODEL = 1024
BATCH = 8
SEQ = 4096
DEPTH = 2

CTX_LEN = 256
GRID_W = 64

HEAD_DIM = 128
N_Q_HEADS = 4
N_KV_HEADS = 2
REP = N_Q_HEADS // N_KV_HEADS
ATTN_W = N_Q_HEADS * HEAD_DIM
KV_W = N_KV_HEADS * HEAD_DIM
AXIS_DIM = HEAD_DIM // 2
ROPE_THETA = 10000.0
Q_BLOCK = 128
SCONV_W = 256
SCONV_K = 3
FOURIER_W = 256
FOURIER_GROUPS = 4
FOURIER_GROUP_W = FOURIER_W // FOURIER_GROUPS
CONF_W = 256
CONF_K = 31
MIX_W = ATTN_W + SCONV_W + FOURIER_W + CONF_W
Q_END = ATTN_W
K_END = Q_END + KV_W
V_END = K_END + KV_W
IN_W = V_END + 3 * SCONV_W + FOURIER_W + 2 * CONF_W
N_EXPERTS = 32
TOP_K = 4
D_FF = D_MODEL
SWIGLU_LIMIT = 7.0
SWIGLU_ALPHA = 1.702
N_MOD = 6
EPS = 1e-6

kernel_name = 'hybrid_parallel_groups_flow_backbone'


def rms_norm(x, g):
    xf = x.astype(jnp.float32)
    y = xf * lax.rsqrt(jnp.mean(xf * xf, axis=-1, keepdims=True) + EPS)
    return (y * g.astype(jnp.float32)).astype(x.dtype)


def layer_norm(x, g, b):
    xf = x.astype(jnp.float32)
    mu = jnp.mean(xf, axis=-1, keepdims=True)
    var = jnp.mean(jnp.square(xf - mu), axis=-1, keepdims=True)
    y = (xf - mu) * lax.rsqrt(var + EPS)
    return (y * g.astype(jnp.float32) + b.astype(jnp.float32)).astype(x.dtype)


def modulate(h, shift, scale):
    return h * (1 + scale) + shift


def axial_rope_tables(rows, dtype):
    row = jnp.broadcast_to(jnp.arange(rows, dtype=jnp.float32)[:, None], (rows, GRID_W)).reshape(-1)
    col = jnp.broadcast_to(jnp.arange(GRID_W, dtype=jnp.float32)[None, :], (rows, GRID_W)).reshape(-1)
    inv_freq = ROPE_THETA ** (-jnp.arange(AXIS_DIM // 2, dtype=jnp.float32) * 2.0 / AXIS_DIM)
    ar = row[:, None] * inv_freq
    ac = col[:, None] * inv_freq
    ang = jnp.concatenate([ar, ar, ac, ac], axis=-1)
    return jnp.cos(ang).astype(dtype), jnp.sin(ang).astype(dtype)


def rotate_half_axial(x):
    xr = x.reshape(x.shape[:-1] + (2, 2, AXIS_DIM // 2))
    x1 = xr[..., 0, :]
    x2 = xr[..., 1, :]
    return jnp.stack([-x2, x1], axis=-2).reshape(x.shape)


def apply_rope(x, cos, sin):
    return x * cos[None, :, None, :] + rotate_half_axial(x) * sin[None, :, None, :]


def split_heads(t, n_heads):
    return t.reshape(t.shape[:-1] + (n_heads, HEAD_DIM))


def attend(qg, k, v):
    s = jnp.einsum('bqgrd,bkgd->bgrqk', qg, k, preferred_element_type=jnp.float32) * (HEAD_DIM ** -0.5)
    p = jax.nn.softmax(s, axis=-1)
    return jnp.einsum('bgrqk,bkgd->bqgrd', p.astype(v.dtype), v)


def block_attention(q, k, v):
    b_, s_ = q.shape[0], q.shape[1]
    n_blk = s_ // Q_BLOCK
    qb = jnp.moveaxis(q.reshape(b_, n_blk, Q_BLOCK, N_KV_HEADS, REP, HEAD_DIM), 1, 0)
    o = lax.map(lambda qblk: attend(qblk, k, v), qb)
    return jnp.moveaxis(o, 0, 1).reshape(b_, s_, ATTN_W)


def depthwise_conv(x, w):
    k_w, ch = w.shape
    return lax.conv_general_dilated(
        x, w[:, None, :], window_strides=(1,), padding=[(k_w // 2, k_w // 2)],
        dimension_numbers=('NWC', 'WIO', 'NWC'), feature_group_count=ch)


def fourier_mix(f):
    b_, s_, _ = f.shape
    fg = f.reshape(b_, s_, FOURIER_GROUPS, FOURIER_GROUP_W).astype(jnp.float32)
    y = jnp.fft.fft2(fg, axes=(1, 3), norm='ortho').real
    return y.reshape(b_, s_, FOURIER_W).astype(f.dtype)


def local_mixers(u_loc, sconv_w, conf_dw_w, conf_dw_b, conf_ln_g, conf_ln_b):
    sb, sc, sx, fo, cf = jnp.split(
        u_loc, [SCONV_W, 2 * SCONV_W, 3 * SCONV_W, 3 * SCONV_W + FOURIER_W], axis=-1)
    y_sconv = sb * depthwise_conv(sc * sx, sconv_w)
    y_four = fourier_mix(fo)
    a, g = jnp.split(cf, 2, axis=-1)
    d = depthwise_conv(a * jax.nn.sigmoid(g), conf_dw_w) + conf_dw_b
    y_conf = jax.nn.silu(layer_norm(d, conf_ln_g, conf_ln_b))
    return jnp.concatenate([y_sconv, y_four, y_conf], axis=-1)


def moe(h, router_w, router_b, w_up, b_up, w_down, b_down):
    shape = h.shape
    ht = h.reshape(-1, D_MODEL)
    logits = (ht @ router_w).astype(jnp.float32) + router_b.astype(jnp.float32)
    top_vals, top_idx = lax.top_k(logits, TOP_K)
    probs = jax.nn.softmax(top_vals, axis=-1)
    combine = jnp.sum(jax.nn.one_hot(top_idx, N_EXPERTS, dtype=jnp.float32) * probs[..., None], axis=1)
    out = jnp.zeros((ht.shape[0], D_MODEL), jnp.float32)
    for e in range(N_EXPERTS):
        gu = ht @ w_up[e] + b_up[e]
        gate = jnp.minimum(gu[:, ::2], SWIGLU_LIMIT)
        lin = jnp.clip(gu[:, 1::2], -SWIGLU_LIMIT, SWIGLU_LIMIT)
        act = gate * jax.nn.sigmoid(SWIGLU_ALPHA * gate) * (lin + 1)
        y = act @ w_down[e] + b_down[e]
        out = out + combine[:, e:e + 1] * y.astype(jnp.float32)
    return out.astype(h.dtype).reshape(shape)


def hybrid_layer(x, xc, c_act, cc_act, cos, sin, w_mod, b_mod, norm_mix, norm_ffn, w_in,
                 q_norm, k_norm, sconv_w, conf_dw_w, conf_dw_b, conf_ln_g, conf_ln_b, w_out,
                 router_w, router_b, w_up, b_up, w_down, b_down, update_ctx):
    b_, s_ = x.shape[0], x.shape[1]
    s_c = xc.shape[1]
    sh1, sc1, g1, sh2, sc2, g2 = jnp.split((c_act @ w_mod + b_mod)[:, None, :], N_MOD, axis=-1)
    csh1, csc1, cg1, csh2, csc2, cg2 = jnp.split((cc_act @ w_mod + b_mod)[None, None, :], N_MOD, axis=-1)

    h = modulate(rms_norm(x, norm_mix), sh1, sc1)
    hc = modulate(rms_norm(xc, norm_mix), csh1, csc1)
    u = h @ w_in
    if update_ctx:
        uc = hc @ w_in
        uc_kv = uc[..., Q_END:V_END]
    else:
        uc_kv = hc @ w_in[:, Q_END:V_END]
    kc = rms_norm(split_heads(uc_kv[..., :KV_W], N_KV_HEADS), k_norm)
    vc = split_heads(uc_kv[..., KV_W:], N_KV_HEADS)
    q = apply_rope(rms_norm(split_heads(u[..., :Q_END], N_Q_HEADS), q_norm), cos, sin)
    k = apply_rope(rms_norm(split_heads(u[..., Q_END:K_END], N_KV_HEADS), k_norm), cos, sin)
    v = split_heads(u[..., K_END:V_END], N_KV_HEADS)
    k_all = jnp.concatenate([kc, k], axis=1)
    v_all = jnp.concatenate([vc, v], axis=1)
    y_attn = block_attention(q, k_all, v_all)
    y = jnp.concatenate(
        [y_attn, local_mixers(u[..., V_END:], sconv_w, conf_dw_w, conf_dw_b, conf_ln_g, conf_ln_b)],
        axis=-1) @ w_out
    x = x + g1 * y
    h2 = modulate(rms_norm(x, norm_ffn), sh2, sc2)

    if update_ctx:
        qc = rms_norm(split_heads(uc[..., :Q_END], N_Q_HEADS), q_norm)
        yc_attn = attend(qc.reshape(b_, s_c, N_KV_HEADS, REP, HEAD_DIM), kc, vc).reshape(b_, s_c, ATTN_W)
        yc = jnp.concatenate(
            [yc_attn, local_mixers(uc[..., V_END:], sconv_w, conf_dw_w, conf_dw_b, conf_ln_g, conf_ln_b)],
            axis=-1) @ w_out
        xc = xc + cg1 * yc
        h2c = modulate(rms_norm(xc, norm_ffn), csh2, csc2)
        ff = moe(jnp.concatenate([h2, h2c], axis=1), router_w, router_b, w_up, b_up, w_down, b_down)
        x = x + g2 * ff[:, :s_]
        xc = xc + cg2 * ff[:, s_:]
    else:
        x = x + g2 * moe(h2, router_w, router_b, w_up, b_up, w_down, b_down)
    return x, xc


def setup_inputs(seed: int = 0) -> dict:
    key = jax.random.key(seed)
    ks = jax.random.split(key, 24)
    L = DEPTH

    def nrm(k, shape, s):
        return s * jax.random.normal(k, shape, jnp.float32)

    return {
        'x': nrm(ks[0], (BATCH, SEQ, D_MODEL), 1.0),
        'c': nrm(ks[1], (BATCH, D_MODEL), 1.0),
        'ctx': nrm(ks[2], (BATCH, CTX_LEN, D_MODEL), 1.0),
        'c_ctx': nrm(ks[3], (D_MODEL,), 1.0),
        'w_mod': nrm(ks[4], (L, D_MODEL, N_MOD * D_MODEL), 0.5 * D_MODEL ** -0.5),
        'b_mod': nrm(ks[5], (L, N_MOD * D_MODEL), 0.01),
        'norm_mix': 1.0 + nrm(ks[6], (L, D_MODEL), 0.05),
        'norm_ffn': 1.0 + nrm(ks[7], (L, D_MODEL), 0.05),
        'w_in': nrm(ks[8], (L, D_MODEL, IN_W), D_MODEL ** -0.5),
        'q_norm': 1.0 + nrm(ks[9], (L, HEAD_DIM), 0.05),
        'k_norm': 1.0 + nrm(ks[10], (L, HEAD_DIM), 0.05),
        'sconv_w': nrm(ks[11], (L, SCONV_K, SCONV_W), SCONV_K ** -0.5),
        'conf_dw_w': nrm(ks[12], (L, CONF_K, CONF_W), CONF_K ** -0.5),
        'conf_dw_b': nrm(ks[13], (L, CONF_W), 0.01),
        'conf_ln_g': 1.0 + nrm(ks[14], (L, CONF_W), 0.05),
        'conf_ln_b': nrm(ks[15], (L, CONF_W), 0.01),
        'w_out': nrm(ks[16], (L, MIX_W, D_MODEL), MIX_W ** -0.5),
        'router_w': nrm(ks[17], (L, D_MODEL, N_EXPERTS), D_MODEL ** -0.5),
        'router_b': nrm(ks[18], (L, N_EXPERTS), 0.01),
        'w_up': nrm(ks[19], (L, N_EXPERTS, D_MODEL, 2 * D_FF), D_MODEL ** -0.5),
        'b_up': nrm(ks[20], (L, N_EXPERTS, 2 * D_FF), 0.01),
        'w_down': nrm(ks[21], (L, N_EXPERTS, D_FF, D_MODEL), D_FF ** -0.5),
        'b_down': nrm(ks[22], (L, N_EXPERTS, D_MODEL), 0.01),
        'final_norm': 1.0 + nrm(ks[23], (D_MODEL,), 0.05),
    }


def reference(x, c, ctx, c_ctx, w_mod, b_mod, norm_mix, norm_ffn, w_in, q_norm, k_norm,
              sconv_w, conf_dw_w, conf_dw_b, conf_ln_g, conf_ln_b, w_out, router_w, router_b,
              w_up, b_up, w_down, b_down, final_norm):
    rows = x.shape[1] // GRID_W
    cos, sin = axial_rope_tables(rows, x.dtype)
    c_act = jax.nn.silu(c)
    cc_act = jax.nn.silu(c_ctx)
    xc = ctx
    for l in range(DEPTH):
        x, xc = hybrid_layer(
            x, xc, c_act, cc_act, cos, sin, w_mod[l], b_mod[l], norm_mix[l], norm_ffn[l], w_in[l],
            q_norm[l], k_norm[l], sconv_w[l], conf_dw_w[l], conf_dw_b[l], conf_ln_g[l], conf_ln_b[l],
            w_out[l], router_w[l], router_b[l], w_up[l], b_up[l], w_down[l], b_down[l],
            update_ctx=(l < DEPTH - 1))
    return rms_norm(x, final_norm)
```

```python
import functools
import math

import jax
import jax.numpy as jnp
from jax import lax
from jax.experimental import pallas as pl
from jax.experimental.pallas import tpu as pltpu

F32 = jnp.float32
BF16 = jnp.bfloat16

HEAD_DIM = 128
N_Q_HEADS = 4
N_KV_HEADS = 2
REP = N_Q_HEADS // N_KV_HEADS
ATTN_W = N_Q_HEADS * HEAD_DIM
KV_W = N_KV_HEADS * HEAD_DIM
AXIS_DIM = HEAD_DIM // 2
ROPE_THETA = 10000.0
GRID_W = 64
SCONV_W = 256
SCONV_K = 3
FOURIER_W = 256
FOURIER_GROUPS = 4
FOURIER_GROUP_W = FOURIER_W // FOURIER_GROUPS
CONF_W = 256
CONF_K = 31
MIX_W = ATTN_W + SCONV_W + FOURIER_W + CONF_W
Q_END = ATTN_W
K_END = Q_END + KV_W
V_END = K_END + KV_W
N_EXPERTS = 32
TOP_K = 4
SWIGLU_LIMIT = 7.0
SWIGLU_ALPHA = 1.702
N_MOD = 6
EPS = 1e-6

LANES = 128
TM = 256
HALO = 16
FF_CHUNK = 256
VMEM_LIMIT = 56 * 1024 * 1024


def _cparams(sem):
    return pltpu.CompilerParams(dimension_semantics=sem, vmem_limit_bytes=VMEM_LIMIT)


def _split_bf16(a):
    hi = a.astype(BF16)
    lo = (a - hi.astype(F32)).astype(BF16)
    return hi, lo


def _dot(a, b):
    return jnp.dot(a, b, preferred_element_type=F32)


def _dot3(a_hi, a_lo, b_hi, b_lo):
    return _dot(a_hi, b_hi) + (_dot(a_lo, b_hi) + _dot(a_hi, b_lo))


def _sigmoid(z):
    return 1.0 / (1.0 + jnp.exp(-z))


def _mod_kernel(c_ref, w_ref, b_ref, o_ref):
    c = c_ref[...]
    act = c * _sigmoid(c)
    a_hi, a_lo = _split_bf16(act)
    w_hi, w_lo = _split_bf16(w_ref[...])
    o_ref[...] = _dot3(a_hi, a_lo, w_hi, w_lo) + b_ref[...]


def _modulation(cvec, w_mod, b_mod):
    r, d = cvec.shape
    n = w_mod.shape[1]
    bn = 1024
    return pl.pallas_call(
        _mod_kernel,
        grid=(n // bn,),
        in_specs=[pl.BlockSpec((r, d), lambda j: (0, 0)),
                  pl.BlockSpec((d, bn), lambda j: (0, j)),
                  pl.BlockSpec((1, bn), lambda j: (0, j))],
        out_specs=pl.BlockSpec((r, bn), lambda j: (0, j)),
        out_shape=jax.ShapeDtypeStruct((r, n), F32),
        compiler_params=_cparams(("arbitrary",)),
        name="modulation",
    )(cvec, w_mod, b_mod.reshape(1, n))


def _in_proj_kernel(x_ref, mod_ref, nw_ref, w_ref, qn_ref, kn_ref, cos_ref, sa_ref, sb_ref,
                    q_ref, k_ref, v_ref, gate_ref, cx_ref, fo_ref, pg_ref):
    x = x_ref[0]
    y = x * lax.rsqrt(jnp.mean(x * x, axis=-1, keepdims=True) + EPS) * nw_ref[...]
    shift = mod_ref[0, 0, 0:1, :]
    scale = mod_ref[0, 0, 1:2, :]
    h = (y * (1.0 + scale) + shift).astype(BF16)

    def proj(a, b):
        return _dot(h, w_ref[:, a:b])

    cos = cos_ref[...]
    sa = sa_ref[...]
    sb = sb_ref[...]

    def norm_rope(t, g):
        n = t * lax.rsqrt(jnp.mean(t * t, axis=-1, keepdims=True) + EPS) * g
        return n * cos + pltpu.roll(n, HEAD_DIM - AXIS_DIM // 2, 1) * sa + pltpu.roll(n, AXIS_DIM // 2, 1) * sb

    uq = proj(0, Q_END)
    qn = qn_ref[...]
    for hh in range(N_Q_HEADS):
        sl = slice(hh * HEAD_DIM, (hh + 1) * HEAD_DIM)
        q_ref[0, :, sl] = (norm_rope(uq[:, sl], qn) * (HEAD_DIM ** -0.5)).astype(BF16)
    uk = proj(Q_END, K_END)
    kn = kn_ref[...]
    for hh in range(N_KV_HEADS):
        sl = slice(hh * HEAD_DIM, (hh + 1) * HEAD_DIM)
        k_ref[0, :, sl] = norm_rope(uk[:, sl], kn).astype(BF16)
    v_ref[0] = proj(K_END, V_END).astype(BF16)
    o = V_END
    gate_ref[0] = proj(o, o + SCONV_W)
    cx_ref[0] = proj(o + SCONV_W, o + 2 * SCONV_W) * proj(o + 2 * SCONV_W, o + 3 * SCONV_W)
    o += 3 * SCONV_W
    fo_ref[0] = proj(o, o + FOURIER_W).astype(BF16)
    o += FOURIER_W
    pg_ref[0] = proj(o, o + CONF_W) * _sigmoid(proj(o + CONF_W, o + 2 * CONF_W))


def _in_proj(x, modsel, norm_w, w_in, q_norm, k_norm, cos, sin_a, sin_b, nct):
    b, t, d = x.shape
    nt = t // TM
    in_w = w_in.shape[1]
    tok = lambda w: pl.BlockSpec((1, TM, w), lambda bi, i: (bi, i, 0))
    full2 = lambda a: pl.BlockSpec(a.shape, lambda bi, i: (0, 0))
    rope = pl.BlockSpec((TM, HEAD_DIM), lambda bi, i: (i, 0))
    outs = [(ATTN_W, BF16), (KV_W, BF16), (KV_W, BF16), (SCONV_W, F32), (SCONV_W, F32),
            (FOURIER_W, BF16), (CONF_W, F32)]
    return pl.pallas_call(
        _in_proj_kernel,
        grid=(b, nt),
        in_specs=[tok(d),
                  pl.BlockSpec((1, 1, N_MOD, d), lambda bi, i: (bi, (i >= nct).astype(jnp.int32), 0, 0)),
                  full2(norm_w),
                  pl.BlockSpec((d, in_w), lambda bi, i: (0, 0)),
                  full2(q_norm), full2(k_norm), rope, rope, rope],
        out_specs=[tok(w) for w, _ in outs],
        out_shape=[jax.ShapeDtypeStruct((b, t, w), dt) for w, dt in outs],
        compiler_params=_cparams(("parallel", "arbitrary")),
        name="in_proj",
    )(x, modsel, norm_w, w_in, q_norm, k_norm, cos, sin_a, sin_b)


def _attn_kernel(q_ref, k_ref, v_ref, o_ref, m_scr, l_scr, acc_scr, *, nct, nt):
    i = pl.program_id(2)
    nkv = jnp.where(i < nct, nct, nt)
    m_scr[...] = jnp.full(m_scr.shape, -jnp.inf, F32)
    l_scr[...] = jnp.zeros(l_scr.shape, F32)
    acc_scr[...] = jnp.zeros(acc_scr.shape, F32)

    def body(j, carry):
        start = pl.multiple_of(j * TM, TM)
        kc = k_ref[0, pl.ds(start, TM), :]
        vc = v_ref[0, pl.ds(start, TM), :]
        for hh in range(REP):
            qh = q_ref[0, :, hh * HEAD_DIM:(hh + 1) * HEAD_DIM]
            s = lax.dot_general(qh, kc, (((1,), (1,)), ((), ())), preferred_element_type=F32)
            m_prev = m_scr[hh]
            m_new = jnp.maximum(m_prev, jnp.max(s, axis=-1, keepdims=True))
            alpha = jnp.exp(m_prev - m_new)
            p = jnp.exp(s - m_new[:, 0:1])
            l_scr[hh] = alpha * l_scr[hh] + jnp.sum(p, axis=-1, keepdims=True)
            acc_scr[hh] = alpha * acc_scr[hh] + _dot(p.astype(BF16), vc)
            m_scr[hh] = m_new
        return carry

    lax.fori_loop(0, nkv, body, 0)
    for hh in range(REP):
        o_ref[0, :, hh * HEAD_DIM:(hh + 1) * HEAD_DIM] = (acc_scr[hh] / l_scr[hh]).astype(BF16)


def _attention(q, k, v, nct):
    b, t, _ = q.shape
    nt = t // TM
    gw = REP * HEAD_DIM
    return pl.pallas_call(
        functools.partial(_attn_kernel, nct=nct, nt=nt),
        grid=(b, N_KV_HEADS, nt),
        in_specs=[pl.BlockSpec((1, TM, gw), lambda bi, g, i: (bi, i, g)),
                  pl.BlockSpec((1, t, HEAD_DIM), lambda bi, g, i: (bi, 0, g)),
                  pl.BlockSpec((1, t, HEAD_DIM), lambda bi, g, i: (bi, 0, g))],
        out_specs=pl.BlockSpec((1, TM, gw), lambda bi, g, i: (bi, i, g)),
        out_shape=jax.ShapeDtypeStruct((b, t, ATTN_W), BF16),
        scratch_shapes=[pltpu.VMEM((REP, TM, HEAD_DIM), F32)] * 3,
        compiler_params=_cparams(("parallel", "parallel", "arbitrary")),
        name="attention",
    )(q, k, v)


def _fourier_kernel(fo_ref, cc_ref, sc_ref, cctx_ref, sctx_ref, clat_ref, slat_ref, o_ref,
                    xc_scr, xs_scr, *, nct, ctx_len):
    i = pl.program_id(1)

    @pl.when(i == 0)
    def _():
        f = fo_ref[0]
        xc_scr[...] = _dot(f, cc_ref[...]).astype(BF16)
        xs_scr[...] = _dot(f, sc_ref[...]).astype(BF16)

    @pl.when(i < nct)
    def _():
        r0 = pl.multiple_of(i * TM, TM)
        o_ref[0] = (_dot(cctx_ref[pl.ds(r0, TM), :], xc_scr[0:ctx_len, :])
                    + _dot(sctx_ref[pl.ds(r0, TM), :], xs_scr[0:ctx_len, :])).astype(BF16)

    @pl.when(i >= nct)
    def _():
        o_ref[0] = (_dot(clat_ref[...], xc_scr[ctx_len:, :])
                    + _dot(slat_ref[...], xs_scr[ctx_len:, :])).astype(BF16)


def _fourier(fo, ccb, scb, cctx, sctx, clat, slat, nct):
    b, t, w = fo.shape
    nt = t // TM
    ctx_len = cctx.shape[0]
    s = clat.shape[0]
    lat = pl.BlockSpec((TM, s), lambda bi, i: (jnp.maximum(i - nct, 0), 0))
    full2 = lambda a: pl.BlockSpec(a.shape, lambda bi, i: (0, 0))
    return pl.pallas_call(
        functools.partial(_fourier_kernel, nct=nct, ctx_len=ctx_len),
        grid=(b, nt),
        in_specs=[pl.BlockSpec((1, t, w), lambda bi, i: (bi, 0, 0)),
                  full2(ccb), full2(scb), full2(cctx), full2(sctx), lat, lat],
        out_specs=pl.BlockSpec((1, TM, w), lambda bi, i: (bi, i, 0)),
        out_shape=jax.ShapeDtypeStruct((b, t, w), BF16),
        scratch_shapes=[pltpu.VMEM((t, w), BF16)] * 2,
        compiler_params=_cparams(("parallel", "arbitrary")),
        name="fourier",
    )(fo, ccb, scb, cctx, sctx, clat, slat)


def _local_kernel(gate_ref, cx_ref, cxp_ref, cxn_ref, pg_ref, pgp_ref, pgn_ref,
                  w3_ref, w31_ref, cb_ref, lg_ref, lb_ref, ysc_ref, ycf_ref, cx_scr, pg_scr,
                  *, nct, nt):
    i = pl.program_id(1)
    has_prev = jnp.logical_and(i != 0, i != nct)
    has_next = jnp.logical_and(i != nct - 1, i != nt - 1)
    pscale = jnp.where(has_prev, 1.0, 0.0).astype(F32)
    nscale = jnp.where(has_next, 1.0, 0.0).astype(F32)
    for src, prv, nxt, scr in ((cx_ref, cxp_ref, cxn_ref, cx_scr), (pg_ref, pgp_ref, pgn_ref, pg_scr)):
        scr[0:HALO, :] = prv[0] * pscale
        scr[HALO:HALO + TM, :] = src[0]
        scr[HALO + TM:, :] = nxt[0] * nscale

    def conv(scr, w_ref, taps):
        off = HALO - taps // 2
        acc = scr[off:off + TM, :] * w_ref[0:1, :]
        for j in range(1, taps):
            acc = acc + scr[off + j:off + j + TM, :] * w_ref[j:j + 1, :]
        return acc

    ysc_ref[0] = (gate_ref[0] * conv(cx_scr, w3_ref, SCONV_K)).astype(BF16)
    d = conv(pg_scr, w31_ref, CONF_K) + cb_ref[...]
    mu = jnp.mean(d, axis=-1, keepdims=True)
    dc = d - mu
    var = jnp.mean(dc * dc, axis=-1, keepdims=True)
    z = dc * lax.rsqrt(var + EPS) * lg_ref[...] + lb_ref[...]
    ycf_ref[0] = (z * _sigmoid(z)).astype(BF16)


def _local(gate, cx, pg, w3, w31, cb, lg, lb, nct):
    b, t, w = gate.shape
    nt = t // TM
    r = TM // HALO
    nh = t // HALO
    tok = pl.BlockSpec((1, TM, w), lambda bi, i: (bi, i, 0))
    prv = pl.BlockSpec((1, HALO, w), lambda bi, i: (bi, jnp.maximum(i * r - 1, 0), 0))
    nxt = pl.BlockSpec((1, HALO, w), lambda bi, i: (bi, jnp.minimum((i + 1) * r, nh - 1), 0))
    full2 = lambda a: pl.BlockSpec(a.shape, lambda bi, i: (0, 0))
    return pl.pallas_call(
        functools.partial(_local_kernel, nct=nct, nt=nt),
        grid=(b, nt),
        in_specs=[tok, tok, prv, nxt, tok, prv, nxt,
                  full2(w3), full2(w31), full2(cb), full2(lg), full2(lb)],
        out_specs=[tok, tok],
        out_shape=[jax.ShapeDtypeStruct((b, t, w), BF16)] * 2,
        scratch_shapes=[pltpu.VMEM((TM + 2 * HALO, w), F32)] * 2,
        compiler_params=_cparams(("parallel", "arbitrary")),
        name="local_mixers",
    )(gate, cx, cx, cx, pg, pg, pg, w3, w31, cb, lg, lb)


def _out_proj_kernel(ya_ref, ysc_ref, yf_ref, ycf_ref, x_ref, mod_ref, nw_ref, w_ref,
                     rwh_ref, rwl_ref, rb_ref, x1_ref, h2_ref, p_ref, e_ref):
    y = (_dot(ya_ref[0], w_ref[0:ATTN_W, :])
         + _dot(ysc_ref[0], w_ref[ATTN_W:ATTN_W + SCONV_W, :])
         + _dot(yf_ref[0], w_ref[ATTN_W + SCONV_W:ATTN_W + SCONV_W + FOURIER_W, :])
         + _dot(ycf_ref[0], w_ref[ATTN_W + SCONV_W + FOURIER_W:, :]))
    x1 = x_ref[0] + mod_ref[0, 0, 2:3, :] * y
    x1_ref[0] = x1
    n = x1 * lax.rsqrt(jnp.mean(x1 * x1, axis=-1, keepdims=True) + EPS) * nw_ref[...]
    h2 = n * (1.0 + mod_ref[0, 0, 4:5, :]) + mod_ref[0, 0, 3:4, :]
    h2_ref[0] = h2

    h_hi, h_lo = _split_bf16(h2)
    lane = lax.broadcasted_iota(jnp.int32, (TM, LANES), 1)
    logits = _dot3(h_hi, h_lo, rwh_ref[...], rwl_ref[...]) + rb_ref[...]
    logits = jnp.where(lane < N_EXPERTS, logits, -jnp.inf)
    vals, idxs = [], []
    for _ in range(TOP_K):
        m = jnp.max(logits, axis=-1, keepdims=True)
        am = jnp.min(jnp.where(logits == m, lane, LANES), axis=-1, keepdims=True)
        vals.append(m)
        idxs.append(am)
        logits = jnp.where(lane == am, -jnp.inf, logits)
    exps = [jnp.exp(v - vals[0]) for v in vals]
    denom = exps[0] + exps[1] + exps[2] + exps[3]
    p_out = jnp.zeros((TM, LANES), F32)
    e_out = jnp.zeros((TM, LANES), jnp.int32)
    for kk in range(TOP_K):
        p_out = jnp.where(lane == kk, exps[kk] / denom, p_out)
        e_out = jnp.where(lane == kk, idxs[kk], e_out)
    p_ref[0] = p_out
    e_ref[0] = e_out


def _out_proj(ya, ysc, yf, ycf, x, modsel, norm_w, w_out, rw_hi, rw_lo, rb, nct):
    b, t, d = x.shape
    nt = t // TM
    tok = lambda w: pl.BlockSpec((1, TM, w), lambda bi, i: (bi, i, 0))
    full2 = lambda a: pl.BlockSpec(a.shape, lambda bi, i: (0, 0))
    return pl.pallas_call(
        _out_proj_kernel,
        grid=(b, nt),
        in_specs=[tok(ATTN_W), tok(SCONV_W), tok(FOURIER_W), tok(CONF_W), tok(d),
                  pl.BlockSpec((1, 1, N_MOD, d), lambda bi, i: (bi, (i >= nct).astype(jnp.int32), 0, 0)),
                  full2(norm_w), full2(w_out), full2(rw_hi), full2(rw_lo), full2(rb)],
        out_specs=[tok(d), tok(d), tok(LANES), tok(LANES)],
        out_shape=[jax.ShapeDtypeStruct((b, t, d), F32), jax.ShapeDtypeStruct((b, t, d), F32),
                   jax.ShapeDtypeStruct((b, t, LANES), F32), jax.ShapeDtypeStruct((b, t, LANES), jnp.int32)],
        compiler_params=_cparams(("parallel", "arbitrary")),
        name="out_proj_router",
    )(ya, ysc, yf, ycf, x, modsel, norm_w, w_out, rw_hi, rw_lo, rb)


def _moe_kernel(te_ref, nu_ref, tok_ref, tokn_ref, dst_ref, h_hbm, wg_ref, wl_ref, bg_ref, bl_ref,
                wd_ref, bd_ref, y_hbm, xbuf, ybuf, gsem, ssem, *, n_tiles, n_spare_start):
    i = pl.program_id(0)
    n_used = nu_ref[0]
    slot = lax.rem(i, 2)

    def gather_rows(idx_ref, sl):
        def body(r, carry):
            t = idx_ref[0, 0, r]
            pltpu.make_async_copy(h_hbm.at[pl.ds(t, 1), :], xbuf.at[sl, pl.ds(r, 1), :], gsem.at[sl]).start()
            return carry
        lax.fori_loop(0, TM, body, 0)

    def wait_tile(buf, sem, sl):
        pltpu.make_async_copy(buf.at[sl], buf.at[sl], sem.at[sl]).wait()

    @pl.when(i == 0)
    def _():
        ybuf[...] = jnp.zeros(ybuf.shape, F32)
        for sl in range(2):
            pltpu.sync_copy(ybuf.at[sl], y_hbm.at[pl.ds(n_spare_start + sl * TM, TM), :])

    @pl.when(jnp.logical_and(i == 0, n_used > 0))
    def _():
        gather_rows(tok_ref, 0)

    @pl.when(i + 1 < n_used)
    def _():
        gather_rows(tokn_ref, 1 - slot)

    @pl.when(jnp.logical_and(i >= 2, i - 2 < n_used))
    def _():
        wait_tile(ybuf, ssem, slot)

    @pl.when(i < n_used)
    def _():
        wait_tile(xbuf, gsem, slot)
        x = xbuf[slot].astype(BF16)
        acts = []
        for c in range(wg_ref.shape[2] // FF_CHUNK):
            cs = slice(c * FF_CHUNK, (c + 1) * FF_CHUNK)
            g = jnp.minimum(_dot(x, wg_ref[0, :, cs]) + bg_ref[0, :, cs], SWIGLU_LIMIT)
            l = jnp.clip(_dot(x, wl_ref[0, :, cs]) + bl_ref[0, :, cs], -SWIGLU_LIMIT, SWIGLU_LIMIT)
            acts.append((g * _sigmoid(SWIGLU_ALPHA * g) * (l + 1.0)).astype(BF16))
        act = jnp.concatenate(acts, axis=-1)
        ybuf[slot] = _dot(act, wd_ref[0]) + bd_ref[0]

        def body(r, carry):
            d = dst_ref[0, 0, r]
            pltpu.make_async_copy(ybuf.at[slot, pl.ds(r, 1), :], y_hbm.at[pl.ds(d, 1), :], ssem.at[slot]).start()
            return carry
        lax.fori_loop(0, TM, body, 0)

    @pl.when(i == n_tiles - 1)
    def _():
        @pl.when(i - 1 < n_used)
        def _():
            wait_tile(ybuf, ssem, 1 - slot)

        @pl.when(i < n_used)
        def _():
            wait_tile(ybuf, ssem, slot)


def _moe(h2, tile_expert, n_used, slot_tok, slot_dst, wg, wl, bg, bl, wd, bd, n_rows_out):
    ntok, d = h2.shape
    n_tiles = slot_tok.shape[0]
    ff = wg.shape[2]
    smem = lambda f: pl.BlockSpec((1, 1, TM), f, memory_space=pltpu.SMEM)
    by_expert = lambda shape: pl.BlockSpec((1,) + shape, lambda i, te, nu: (te[i], 0, 0))
    grid_spec = pltpu.PrefetchScalarGridSpec(
        num_scalar_prefetch=2,
        grid=(n_tiles,),
        in_specs=[smem(lambda i, te, nu: (i, 0, 0)),
                  smem(lambda i, te, nu: (jnp.minimum(i + 1, n_tiles - 1), 0, 0)),
                  smem(lambda i, te, nu: (i, 0, 0)),
                  pl.BlockSpec(memory_space=pl.ANY),
                  by_expert((d, ff)), by_expert((d, ff)), by_expert((1, ff)), by_expert((1, ff)),
                  by_expert((ff, d)), by_expert((1, d))],
        out_specs=pl.BlockSpec(memory_space=pl.ANY),
        scratch_shapes=[pltpu.VMEM((2, TM, d), F32), pltpu.VMEM((2, TM, d), F32),
                        pltpu.SemaphoreType.DMA((2,)), pltpu.SemaphoreType.DMA((2,))],
    )
    return pl.pallas_call(
        functools.partial(_moe_kernel, n_tiles=n_tiles, n_spare_start=n_rows_out - 2 * TM),
        grid_spec=grid_spec,
        out_shape=jax.ShapeDtypeStruct((n_rows_out, d), F32),
        compiler_params=_cparams(("arbitrary",)),
        name="moe_experts",
    )(tile_expert, n_used, slot_tok, slot_tok, slot_dst, h2, wg, wl, bg, bl, wd, bd)


def _dispatch_plan(top_e, ntok):
    a = ntok * TOP_K
    n_tiles = a // TM + N_EXPERTS
    e_flat = top_e.reshape(a)
    order = jnp.argsort(e_flat, stable=True).astype(jnp.int32)
    e_sorted = e_flat[order]
    experts = jnp.arange(N_EXPERTS, dtype=jnp.int32)
    cstart = jnp.searchsorted(e_sorted, experts, side="left").astype(jnp.int32)
    cend = jnp.searchsorted(e_sorted, experts, side="right").astype(jnp.int32)
    counts = cend - cstart
    padded = ((counts + TM - 1) // TM) * TM
    gend = jnp.cumsum(padded).astype(jnp.int32)
    gstart = gend - padded
    n_used = gend[-1] // TM
    tiles = jnp.arange(n_tiles, dtype=jnp.int32)
    te_raw = jnp.searchsorted(gend, tiles * TM, side="right").astype(jnp.int32)
    te_raw = jnp.minimum(te_raw, N_EXPERTS - 1)
    last = te_raw[jnp.maximum(n_used - 1, 0)]
    tile_expert = jnp.where(tiles < n_used, te_raw, last)
    slots = jnp.arange(n_tiles * TM, dtype=jnp.int32)
    s_tile = slots // TM
    s_e = te_raw[s_tile]
    rank = slots - gstart[s_e]
    valid = jnp.logical_and(rank < counts[s_e], s_tile < n_used)
    asg = order[jnp.clip(cstart[s_e] + rank, 0, a - 1)]
    slot_tok = jnp.where(valid, asg // TOP_K, 0)
    slot_dst = jnp.where(valid, asg, a + (s_tile % 2) * TM + slots % TM)
    return (tile_expert, n_used.reshape(1).astype(jnp.int32),
            slot_tok.reshape(n_tiles, 1, TM), slot_dst.reshape(n_tiles, 1, TM))


def _combine_kernel(x_ref, y_ref, p_ref, mod_ref, fn_ref, o_ref, *, final):
    d = x_ref.shape[2]
    p = p_ref[0]
    ff = p[:, 0:1] * y_ref[:, 0:d]
    for kk in range(1, TOP_K):
        ff = ff + p[:, kk:kk + 1] * y_ref[:, kk * d:(kk + 1) * d]
    x2 = x_ref[0] + mod_ref[0, 0, 5:6, :] * ff
    if final:
        x2 = x2 * lax.rsqrt(jnp.mean(x2 * x2, axis=-1, keepdims=True) + EPS) * fn_ref[...]
    o_ref[0] = x2


def _combine(x1, y4, probs, modsel, final_norm, nct, final):
    b, t, d = x1.shape
    nt = t // TM
    first = nct if final else 0
    n_out = nt - first
    return pl.pallas_call(
        functools.partial(_combine_kernel, final=final),
        grid=(b, n_out),
        in_specs=[pl.BlockSpec((1, TM, d), lambda bi, i: (bi, i + first, 0)),
                  pl.BlockSpec((TM, TOP_K * d), lambda bi, i: (bi * nt + i + first, 0)),
                  pl.BlockSpec((1, TM, LANES), lambda bi, i: (bi, i + first, 0)),
                  pl.BlockSpec((1, 1, N_MOD, d), lambda bi, i: (bi, (i + first >= nct).astype(jnp.int32), 0, 0)),
                  pl.BlockSpec((1, d), lambda bi, i: (0, 0))],
        out_specs=pl.BlockSpec((1, TM, d), lambda bi, i: (bi, i, 0)),
        out_shape=jax.ShapeDtypeStruct((b, n_out * TM, d), F32),
        compiler_params=_cparams(("parallel", "arbitrary")),
        name="combine_final" if final else "combine",
    )(x1, y4, probs, modsel, final_norm)


def _rope_tables(ctx_len, rows):
    row = jnp.broadcast_to(jnp.arange(rows, dtype=F32)[:, None], (rows, GRID_W)).reshape(-1)
    col = jnp.broadcast_to(jnp.arange(GRID_W, dtype=F32)[None, :], (rows, GRID_W)).reshape(-1)
    inv_freq = ROPE_THETA ** (-jnp.arange(AXIS_DIM // 2, dtype=F32) * 2.0 / AXIS_DIM)
    ar = row[:, None] * inv_freq
    ac = col[:, None] * inv_freq
    ang = jnp.concatenate([ar, ar, ac, ac], axis=-1)
    cos = jnp.concatenate([jnp.ones((ctx_len, HEAD_DIM), F32), jnp.cos(ang)], axis=0)
    sin = jnp.concatenate([jnp.zeros((ctx_len, HEAD_DIM), F32), jnp.sin(ang)], axis=0)
    first_half = (jnp.arange(HEAD_DIM) % AXIS_DIM) < AXIS_DIM // 2
    return cos, jnp.where(first_half, -sin, 0.0), jnp.where(first_half, 0.0, sin)


def _dft_tables(n, width):
    k = jnp.arange(n, dtype=jnp.int32)
    ang = ((k[:, None] * k[None, :]) % n).astype(F32) * (2.0 * math.pi / n)
    scale = 1.0 / math.sqrt(n * width)
    return (jnp.cos(ang) * scale).astype(BF16), (-jnp.sin(ang) * scale).astype(BF16)


def _channel_dft_tables():
    c = jnp.arange(FOURIER_W, dtype=jnp.int32)
    same = (c[:, None] // FOURIER_GROUP_W) == (c[None, :] // FOURIER_GROUP_W)
    kk = (c[:, None] % FOURIER_GROUP_W) * (c[None, :] % FOURIER_GROUP_W) % FOURIER_GROUP_W
    ang = kk.astype(F32) * (2.0 * math.pi / FOURIER_GROUP_W)
    return (jnp.where(same, jnp.cos(ang), 0.0).astype(BF16), jnp.where(same, jnp.sin(ang), 0.0).astype(BF16))


def kernel(x, c, ctx, c_ctx, w_mod, b_mod, norm_mix, norm_ffn, w_in, q_norm, k_norm, sconv_w,
           conf_dw_w, conf_dw_b, conf_ln_g, conf_ln_b, w_out, router_w, router_b, w_up, b_up,
           w_down, b_down, final_norm):
    b, s, d = x.shape
    ctx_len = ctx.shape[1]
    depth = w_mod.shape[0]
    assert ctx_len % TM == 0 and s % TM == 0 and s % GRID_W == 0
    t = ctx_len + s
    nct = ctx_len // TM
    ntok = b * t
    n_asg = ntok * TOP_K

    cos, sin_a, sin_b = _rope_tables(ctx_len, s // GRID_W)
    ccb, scb = _channel_dft_tables()
    cctx, sctx = _dft_tables(ctx_len, FOURIER_GROUP_W)
    clat, slat = _dft_tables(s, FOURIER_GROUP_W)

    rows = ((b + 1 + 7) // 8) * 8
    cvec = jnp.zeros((rows, d), F32).at[:b].set(c).at[b].set(c_ctx)

    xa = jnp.concatenate([ctx, x], axis=1)
    out = None
    for l in range(depth):
        mod = _modulation(cvec, w_mod[l], b_mod[l]).reshape(rows, N_MOD, d)
        modsel = jnp.stack([jnp.broadcast_to(mod[b], (b, N_MOD, d)), mod[:b]], axis=1)

        q, k, v, gate, cx, fo, pg = _in_proj(
            xa, modsel, norm_mix[l].reshape(1, d), w_in[l].astype(BF16),
            q_norm[l].reshape(1, HEAD_DIM), k_norm[l].reshape(1, HEAD_DIM), cos, sin_a, sin_b, nct)
        ya = _attention(q, k, v, nct)
        yf = _fourier(fo, ccb, scb, cctx, sctx, clat, slat, nct)
        ysc, ycf = _local(gate, cx, pg, sconv_w[l], conf_dw_w[l], conf_dw_b[l].reshape(1, CONF_W),
                          conf_ln_g[l].reshape(1, CONF_W), conf_ln_b[l].reshape(1, CONF_W), nct)

        rw = jnp.zeros((d, LANES), F32).at[:, :N_EXPERTS].set(router_w[l])
        rw_hi, rw_lo = _split_bf16(rw)
        rb = jnp.zeros((1, LANES), F32).at[0, :N_EXPERTS].set(router_b[l])
        x1, h2, probs, top_e = _out_proj(ya, ysc, yf, ycf, xa, modsel, norm_ffn[l].reshape(1, d),
                                         w_out[l].astype(BF16), rw_hi, rw_lo, rb, nct)

        tile_expert, n_used, slot_tok, slot_dst = _dispatch_plan(top_e.reshape(ntok, LANES)[:, :TOP_K], ntok)
        wg = w_up[l][:, :, 0::2].astype(BF16)
        wl = w_up[l][:, :, 1::2].astype(BF16)
        bg = b_up[l][:, None, 0::2]
        bl = b_up[l][:, None, 1::2]
        y4 = _moe(h2.reshape(ntok, d), tile_expert, n_used, slot_tok, slot_dst, wg, wl, bg, bl,
                  w_down[l].astype(BF16), b_down[l][:, None, :], n_asg + 2 * TM)
        y4 = y4.reshape((n_asg + 2 * TM) // TOP_K, TOP_K * d)

        final = l == depth - 1
        res = _combine(x1, y4, probs, modsel, final_norm.reshape(1, d), nct, final)
        if final:
            out = res
        else:
            xa = res
    return out
```

```python
import functools
import math

import jax
import jax.numpy as jnp
from jax import lax
from jax.experimental import pallas as pl
from jax.experimental.pallas import tpu as pltpu

F32 = jnp.float32
BF16 = jnp.bfloat16

HEAD_DIM = 128
N_Q_HEADS = 4
N_KV_HEADS = 2
REP = N_Q_HEADS // N_KV_HEADS
ATTN_W = N_Q_HEADS * HEAD_DIM
KV_W = N_KV_HEADS * HEAD_DIM
AXIS_DIM = HEAD_DIM // 2
ROPE_THETA = 10000.0
GRID_W = 64
SCONV_W = 256
SCONV_K = 3
FOURIER_W = 256
FOURIER_GROUPS = 4
FOURIER_GROUP_W = FOURIER_W // FOURIER_GROUPS
CONF_W = 256
CONF_K = 31
MIX_W = ATTN_W + SCONV_W + FOURIER_W + CONF_W
Q_END = ATTN_W
K_END = Q_END + KV_W
V_END = K_END + KV_W
N_EXPERTS = 32
TOP_K = 4
SWIGLU_LIMIT = 7.0
SWIGLU_ALPHA = 1.702
N_MOD = 6
EPS = 1e-6

LANES = 128
TM = 256
HALO = 16
FF_CHUNK = 256
VMEM_LIMIT = 56 * 1024 * 1024


def _cparams(sem):
    return pltpu.CompilerParams(dimension_semantics=sem, vmem_limit_bytes=VMEM_LIMIT)


def _split_bf16(a):
    hi = a.astype(BF16)
    lo = (a - hi.astype(F32)).astype(BF16)
    return hi, lo


def _dot(a, b):
    return jnp.dot(a, b, preferred_element_type=F32)


def _dot3(a_hi, a_lo, b_hi, b_lo):
    return _dot(a_hi, b_hi) + (_dot(a_lo, b_hi) + _dot(a_hi, b_lo))


def _sigmoid(z):
    return 1.0 / (1.0 + jnp.exp(-z))


def _mod_kernel(c_ref, w_ref, b_ref, o_ref):
    c = c_ref[...]
    act = c * _sigmoid(c)
    a_hi, a_lo = _split_bf16(act)
    w_hi, w_lo = _split_bf16(w_ref[...])
    o_ref[...] = _dot3(a_hi, a_lo, w_hi, w_lo) + b_ref[...]


def _modulation(cvec, w_mod, b_mod):
    r, d = cvec.shape
    n = w_mod.shape[1]
    bn = 1024
    return pl.pallas_call(
        _mod_kernel,
        grid=(n // bn,),
        in_specs=[pl.BlockSpec((r, d), lambda j: (0, 0)),
                  pl.BlockSpec((d, bn), lambda j: (0, j)),
                  pl.BlockSpec((1, bn), lambda j: (0, j))],
        out_specs=pl.BlockSpec((r, bn), lambda j: (0, j)),
        out_shape=jax.ShapeDtypeStruct((r, n), F32),
        compiler_params=_cparams(("arbitrary",)),
        name="modulation",
    )(cvec, w_mod, b_mod.reshape(1, n))


def _in_proj_kernel(x_ref, mod_ref, nw_ref, w_ref, qn_ref, kn_ref, cos_ref, sa_ref, sb_ref,
                    q_ref, k_ref, v_ref, gate_ref, cx_ref, fo_ref, pg_ref):
    x = x_ref[0]
    y = x * lax.rsqrt(jnp.mean(x * x, axis=-1, keepdims=True) + EPS) * nw_ref[...]
    shift = mod_ref[0, 0, 0:1, :]
    scale = mod_ref[0, 0, 1:2, :]
    h = (y * (1.0 + scale) + shift).astype(BF16)

    def proj(a, b):
        return _dot(h, w_ref[:, a:b])

    cos = cos_ref[...]
    sa = sa_ref[...]
    sb = sb_ref[...]

    def norm_rope(t, g):
        n = t * lax.rsqrt(jnp.mean(t * t, axis=-1, keepdims=True) + EPS) * g
        return n * cos + pltpu.roll(n, HEAD_DIM - AXIS_DIM // 2, 1) * sa + pltpu.roll(n, AXIS_DIM // 2, 1) * sb

    uq = proj(0, Q_END)
    qn = qn_ref[...]
    for hh in range(N_Q_HEADS):
        sl = slice(hh * HEAD_DIM, (hh + 1) * HEAD_DIM)
        q_ref[0, :, sl] = (norm_rope(uq[:, sl], qn) * (HEAD_DIM ** -0.5)).astype(BF16)
    uk = proj(Q_END, K_END)
    kn = kn_ref[...]
    for hh in range(N_KV_HEADS):
        sl = slice(hh * HEAD_DIM, (hh + 1) * HEAD_DIM)
        k_ref[0, hh] = norm_rope(uk[:, sl], kn).T.astype(BF16)
    v_ref[0] = proj(K_END, V_END).astype(BF16)
    o = V_END
    gate_ref[0] = proj(o, o + SCONV_W)
    cx_ref[0] = proj(o + SCONV_W, o + 2 * SCONV_W) * proj(o + 2 * SCONV_W, o + 3 * SCONV_W)
    o += 3 * SCONV_W
    fo_ref[0] = proj(o, o + FOURIER_W).astype(BF16)
    o += FOURIER_W
    pg_ref[0] = proj(o, o + CONF_W) * _sigmoid(proj(o + CONF_W, o + 2 * CONF_W))


def _in_proj(x, modsel, norm_w, w_in, q_norm, k_norm, cos, sin_a, sin_b, nct):
    b, t, d = x.shape
    nt = t // TM
    in_w = w_in.shape[1]
    tok = lambda w: pl.BlockSpec((1, TM, w), lambda bi, i: (bi, i, 0))
    full2 = lambda a: pl.BlockSpec(a.shape, lambda bi, i: (0, 0))
    rope = pl.BlockSpec((TM, HEAD_DIM), lambda bi, i: (i, 0))
    outs = [(ATTN_W, BF16), (KV_W, BF16), (KV_W, BF16), (SCONV_W, F32), (SCONV_W, F32),
            (FOURIER_W, BF16), (CONF_W, F32)]
    out_specs = [tok(w) for w, _ in outs]
    out_shape = [jax.ShapeDtypeStruct((b, t, w), dt) for w, dt in outs]
    out_specs[1] = pl.BlockSpec((1, N_KV_HEADS, HEAD_DIM, TM), lambda bi, i: (bi, 0, 0, i))
    out_shape[1] = jax.ShapeDtypeStruct((b, N_KV_HEADS, HEAD_DIM, t), BF16)
    return pl.pallas_call(
        _in_proj_kernel,
        grid=(b, nt),
        in_specs=[tok(d),
                  pl.BlockSpec((1, 1, N_MOD, d), lambda bi, i: (bi, (i >= nct).astype(jnp.int32), 0, 0)),
                  full2(norm_w),
                  pl.BlockSpec((d, in_w), lambda bi, i: (0, 0)),
                  full2(q_norm), full2(k_norm), rope, rope, rope],
        out_specs=out_specs,
        out_shape=out_shape,
        compiler_params=_cparams(("parallel", "arbitrary")),
        name="in_proj",
    )(x, modsel, norm_w, w_in, q_norm, k_norm, cos, sin_a, sin_b)


def _attn_kernel(q_ref, kt_ref, v_ref, o_ref, *, nct, ctx_len):
    i = pl.program_id(2)

    def run(nk):
        kt = kt_ref[0, 0, :, 0:nk]
        vv = v_ref[0, 0:nk, :]
        for hh in range(REP):
            sl = slice(hh * HEAD_DIM, (hh + 1) * HEAD_DIM)
            s = _dot(q_ref[0, :, sl], kt)
            p = jnp.exp(s - jnp.max(s, axis=-1, keepdims=True))
            l = jnp.sum(p, axis=-1, keepdims=True)
            o_ref[0, :, sl] = (_dot(p.astype(BF16), vv) / l).astype(BF16)

    @pl.when(i < nct)
    def _():
        run(ctx_len)

    @pl.when(i >= nct)
    def _():
        run(kt_ref.shape[3])


def _attention(q, kt, v, nct):
    b, t, _ = q.shape
    nt = t // TM
    gw = REP * HEAD_DIM
    return pl.pallas_call(
        functools.partial(_attn_kernel, nct=nct, ctx_len=nct * TM),
        grid=(b, N_KV_HEADS, nt),
        in_specs=[pl.BlockSpec((1, TM, gw), lambda bi, g, i: (bi, i, g)),
                  pl.BlockSpec((1, 1, HEAD_DIM, t), lambda bi, g, i: (bi, g, 0, 0)),
                  pl.BlockSpec((1, t, HEAD_DIM), lambda bi, g, i: (bi, 0, g))],
        out_specs=pl.BlockSpec((1, TM, gw), lambda bi, g, i: (bi, i, g)),
        out_shape=jax.ShapeDtypeStruct((b, t, ATTN_W), BF16),
        compiler_params=_cparams(("parallel", "parallel", "arbitrary")),
        name="attention",
    )(q, kt, v)


def _fourier_kernel(fo_ref, cc_ref, sc_ref, cctx_ref, sctx_ref, clat_ref, slat_ref, o_ref,
                    xc_scr, xs_scr, *, nct, ctx_len):
    i = pl.program_id(1)

    @pl.when(i == 0)
    def _():
        f = fo_ref[0]
        xc_scr[...] = _dot(f, cc_ref[...]).astype(BF16)
        xs_scr[...] = _dot(f, sc_ref[...]).astype(BF16)

    @pl.when(i < nct)
    def _():
        r0 = pl.multiple_of(i * TM, TM)
        o_ref[0] = (_dot(cctx_ref[pl.ds(r0, TM), :], xc_scr[0:ctx_len, :])
                    + _dot(sctx_ref[pl.ds(r0, TM), :], xs_scr[0:ctx_len, :])).astype(BF16)

    @pl.when(i >= nct)
    def _():
        o_ref[0] = (_dot(clat_ref[...], xc_scr[ctx_len:, :])
                    + _dot(slat_ref[...], xs_scr[ctx_len:, :])).astype(BF16)


def _fourier(fo, ccb, scb, cctx, sctx, clat, slat, nct):
    b, t, w = fo.shape
    nt = t // TM
    ctx_len = cctx.shape[0]
    s = clat.shape[0]
    lat = pl.BlockSpec((TM, s), lambda bi, i: (jnp.maximum(i - nct, 0), 0))
    full2 = lambda a: pl.BlockSpec(a.shape, lambda bi, i: (0, 0))
    return pl.pallas_call(
        functools.partial(_fourier_kernel, nct=nct, ctx_len=ctx_len),
        grid=(b, nt),
        in_specs=[pl.BlockSpec((1, t, w), lambda bi, i: (bi, 0, 0)),
                  full2(ccb), full2(scb), full2(cctx), full2(sctx), lat, lat],
        out_specs=pl.BlockSpec((1, TM, w), lambda bi, i: (bi, i, 0)),
        out_shape=jax.ShapeDtypeStruct((b, t, w), BF16),
        scratch_shapes=[pltpu.VMEM((t, w), BF16)] * 2,
        compiler_params=_cparams(("parallel", "arbitrary")),
        name="fourier",
    )(fo, ccb, scb, cctx, sctx, clat, slat)


def _local_kernel(gate_ref, cx_ref, cxp_ref, cxn_ref, pg_ref, pgp_ref, pgn_ref,
                  w3_ref, w31_ref, cb_ref, lg_ref, lb_ref, ysc_ref, ycf_ref, cx_scr, pg_scr,
                  *, nct, nt):
    i = pl.program_id(1)
    has_prev = jnp.logical_and(i != 0, i != nct)
    has_next = jnp.logical_and(i != nct - 1, i != nt - 1)
    pscale = jnp.where(has_prev, 1.0, 0.0).astype(F32)
    nscale = jnp.where(has_next, 1.0, 0.0).astype(F32)
    for src, prv, nxt, scr in ((cx_ref, cxp_ref, cxn_ref, cx_scr), (pg_ref, pgp_ref, pgn_ref, pg_scr)):
        scr[0:HALO, :] = prv[0] * pscale
        scr[HALO:HALO + TM, :] = src[0]
        scr[HALO + TM:, :] = nxt[0] * nscale

    def conv(scr, w_ref, taps):
        off = HALO - taps // 2
        acc = scr[off:off + TM, :] * w_ref[0:1, :]
        for j in range(1, taps):
            acc = acc + scr[off + j:off + j + TM, :] * w_ref[j:j + 1, :]
        return acc

    ysc_ref[0] = (gate_ref[0] * conv(cx_scr, w3_ref, SCONV_K)).astype(BF16)
    d = conv(pg_scr, w31_ref, CONF_K) + cb_ref[...]
    mu = jnp.mean(d, axis=-1, keepdims=True)
    dc = d - mu
    var = jnp.mean(dc * dc, axis=-1, keepdims=True)
    z = dc * lax.rsqrt(var + EPS) * lg_ref[...] + lb_ref[...]
    ycf_ref[0] = (z * _sigmoid(z)).astype(BF16)


def _local(gate, cx, pg, w3, w31, cb, lg, lb, nct):
    b, t, w = gate.shape
    nt = t // TM
    r = TM // HALO
    nh = t // HALO
    tok = pl.BlockSpec((1, TM, w), lambda bi, i: (bi, i, 0))
    prv = pl.BlockSpec((1, HALO, w), lambda bi, i: (bi, jnp.maximum(i * r - 1, 0), 0))
    nxt = pl.BlockSpec((1, HALO, w), lambda bi, i: (bi, jnp.minimum((i + 1) * r, nh - 1), 0))
    full2 = lambda a: pl.BlockSpec(a.shape, lambda bi, i: (0, 0))
    return pl.pallas_call(
        functools.partial(_local_kernel, nct=nct, nt=nt),
        grid=(b, nt),
        in_specs=[tok, tok, prv, nxt, tok, prv, nxt,
                  full2(w3), full2(w31), full2(cb), full2(lg), full2(lb)],
        out_specs=[tok, tok],
        out_shape=[jax.ShapeDtypeStruct((b, t, w), BF16)] * 2,
        scratch_shapes=[pltpu.VMEM((TM + 2 * HALO, w), F32)] * 2,
        compiler_params=_cparams(("parallel", "arbitrary")),
        name="local_mixers",
    )(gate, cx, cx, cx, pg, pg, pg, w3, w31, cb, lg, lb)


def _out_proj_kernel(ya_ref, ysc_ref, yf_ref, ycf_ref, x_ref, mod_ref, nw_ref, w_ref,
                     rwh_ref, rwl_ref, rb_ref, tri_ref, x1_ref, h2_ref, p_ref, e_ref, r_ref, cnt_ref,
                     base_scr):
    y = (_dot(ya_ref[0], w_ref[0:ATTN_W, :])
         + _dot(ysc_ref[0], w_ref[ATTN_W:ATTN_W + SCONV_W, :])
         + _dot(yf_ref[0], w_ref[ATTN_W + SCONV_W:ATTN_W + SCONV_W + FOURIER_W, :])
         + _dot(ycf_ref[0], w_ref[ATTN_W + SCONV_W + FOURIER_W:, :]))
    x1 = x_ref[0] + mod_ref[0, 0, 2:3, :] * y
    x1_ref[0] = x1
    n = x1 * lax.rsqrt(jnp.mean(x1 * x1, axis=-1, keepdims=True) + EPS) * nw_ref[...]
    h2 = n * (1.0 + mod_ref[0, 0, 4:5, :]) + mod_ref[0, 0, 3:4, :]
    h2_ref[0] = h2

    h_hi, h_lo = _split_bf16(h2)
    lane = lax.broadcasted_iota(jnp.int32, (TM, LANES), 1)
    logits = _dot3(h_hi, h_lo, rwh_ref[...], rwl_ref[...]) + rb_ref[...]
    logits = jnp.where(lane < N_EXPERTS, logits, -jnp.inf)
    vals, idxs = [], []
    for _ in range(TOP_K):
        m = jnp.max(logits, axis=-1, keepdims=True)
        am = jnp.min(jnp.where(logits == m, lane, LANES), axis=-1, keepdims=True)
        vals.append(m)
        idxs.append(am)
        logits = jnp.where(lane == am, -jnp.inf, logits)
    exps = [jnp.exp(v - vals[0]) for v in vals]
    denom = exps[0] + exps[1] + exps[2] + exps[3]
    @pl.when(jnp.logical_and(pl.program_id(0) == 0, pl.program_id(1) == 0))
    def _():
        base_scr[...] = jnp.zeros(base_scr.shape, F32)

    hits = [lane == idxs[kk] for kk in range(TOP_K)]
    cnt = jnp.zeros((TM, LANES), F32)
    for kk in range(TOP_K):
        cnt = cnt + jnp.where(hits[kk], 1.0, 0.0)
    before = _dot(tri_ref[...], cnt.astype(BF16)) + base_scr[0:1, :]
    base_scr[...] = base_scr[...] + jnp.sum(cnt, axis=0, keepdims=True)
    cnt_ref[...] = base_scr[...]

    p_out = jnp.zeros((TM, LANES), F32)
    e_out = jnp.zeros((TM, LANES), jnp.int32)
    r_out = jnp.zeros((TM, LANES), jnp.int32)
    for kk in range(TOP_K):
        rank = jnp.sum(jnp.where(hits[kk], before, 0.0), axis=-1, keepdims=True).astype(jnp.int32)
        p_out = jnp.where(lane == kk, exps[kk] / denom, p_out)
        e_out = jnp.where(lane == kk, idxs[kk], e_out)
        r_out = jnp.where(lane == kk, rank, r_out)
    p_ref[0] = p_out
    e_ref[0] = e_out
    r_ref[0] = r_out


def _out_proj(ya, ysc, yf, ycf, x, modsel, norm_w, w_out, rw_hi, rw_lo, rb, nct):
    b, t, d = x.shape
    nt = t // TM
    tok = lambda w: pl.BlockSpec((1, TM, w), lambda bi, i: (bi, i, 0))
    full2 = lambda a: pl.BlockSpec(a.shape, lambda bi, i: (0, 0))
    tri = (lax.broadcasted_iota(jnp.int32, (TM, TM), 1) < lax.broadcasted_iota(jnp.int32, (TM, TM), 0)).astype(BF16)
    i32 = jnp.int32
    return pl.pallas_call(
        _out_proj_kernel,
        grid=(b, nt),
        in_specs=[tok(ATTN_W), tok(SCONV_W), tok(FOURIER_W), tok(CONF_W), tok(d),
                  pl.BlockSpec((1, 1, N_MOD, d), lambda bi, i: (bi, (i >= nct).astype(jnp.int32), 0, 0)),
                  full2(norm_w), full2(w_out), full2(rw_hi), full2(rw_lo), full2(rb), full2(tri)],
        out_specs=[tok(d), tok(d), tok(LANES), tok(LANES), tok(LANES),
                   pl.BlockSpec((8, LANES), lambda bi, i: (0, 0))],
        out_shape=[jax.ShapeDtypeStruct((b, t, d), F32), jax.ShapeDtypeStruct((b, t, d), F32),
                   jax.ShapeDtypeStruct((b, t, LANES), F32), jax.ShapeDtypeStruct((b, t, LANES), i32),
                   jax.ShapeDtypeStruct((b, t, LANES), i32), jax.ShapeDtypeStruct((8, LANES), F32)],
        scratch_shapes=[pltpu.VMEM((8, LANES), F32)],
        compiler_params=_cparams(("arbitrary", "arbitrary")),
        name="out_proj_router",
    )(ya, ysc, yf, ycf, x, modsel, norm_w, w_out, rw_hi, rw_lo, rb, tri)


def _deinterleave_kernel(w_ref, pe_ref, po_ref, g_ref, l_ref):
    w = w_ref[0, 0].astype(BF16)
    g_ref[0] = _dot(w, pe_ref[...]).astype(BF16)
    l_ref[0] = _dot(w, po_ref[...]).astype(BF16)


def _deinterleave_up(w_up, layer):
    _, e, d, f2 = w_up.shape
    wb = 2 * FF_CHUNK
    rows = lax.broadcasted_iota(jnp.int32, (wb, FF_CHUNK), 0)
    cols = lax.broadcasted_iota(jnp.int32, (wb, FF_CHUNK), 1)
    pick_even = (rows == 2 * cols).astype(BF16)
    pick_odd = (rows == 2 * cols + 1).astype(BF16)
    out = jax.ShapeDtypeStruct((e, d, f2 // 2), BF16)
    return pl.pallas_call(
        _deinterleave_kernel,
        grid=(e, f2 // wb),
        in_specs=[pl.BlockSpec((1, 1, d, wb), lambda ei, j: (layer, ei, 0, j)),
                  pl.BlockSpec((wb, FF_CHUNK), lambda ei, j: (0, 0)),
                  pl.BlockSpec((wb, FF_CHUNK), lambda ei, j: (0, 0))],
        out_specs=[pl.BlockSpec((1, d, FF_CHUNK), lambda ei, j: (ei, 0, j))] * 2,
        out_shape=[out, out],
        compiler_params=_cparams(("parallel", "arbitrary")),
        name="deinterleave_up",
    )(w_up, pick_even, pick_odd)


def _dispatch_kernel(ps_ref, pn_ref, nu_ref, dest_ref, h_ref, xs_hbm, stage, sem, zsem, *, n_steps, n_tiles):
    i = pl.program_id(0)
    slot = lax.rem(i, 2)

    def wait_slot(sl):
        for _ in range(TOP_K):
            pltpu.make_async_copy(stage.at[sl], stage.at[sl], sem.at[sl]).wait()

    @pl.when(i == 0)
    def _():
        stage[0] = jnp.zeros(stage.shape[1:], F32)

        def per_expert(e, carry):
            start = ps_ref[e]
            n = pn_ref[e]

            def issue(r, c):
                pltpu.make_async_copy(stage.at[0, pl.ds(r, 1), :], xs_hbm.at[pl.ds(start + r, 1), :], zsem).start()
                return c

            def drain(r, c):
                pltpu.make_async_copy(stage.at[0, pl.ds(r, 1), :], xs_hbm.at[pl.ds(start + r, 1), :], zsem).wait()
                return c

            lax.fori_loop(0, n, issue, 0)
            lax.fori_loop(0, n, drain, 0)
            return carry

        lax.fori_loop(0, N_EXPERTS, per_expert, 0)

        def unused_tile(j, carry):
            cp = pltpu.make_async_copy(stage.at[0], xs_hbm.at[pl.ds(pl.multiple_of(j * TM, TM), TM), :], zsem)
            cp.start()
            cp.wait()
            return carry

        lax.fori_loop(nu_ref[0], n_tiles, unused_tile, 0)

    @pl.when(i >= 2)
    def _():
        wait_slot(slot)

    stage[slot] = h_ref[...]

    def body(r, carry):
        for kk in range(TOP_K):
            dd = dest_ref[0, 0, r * TOP_K + kk]
            pltpu.make_async_copy(stage.at[slot, pl.ds(r, 1), :], xs_hbm.at[pl.ds(dd, 1), :], sem.at[slot]).start()
        return carry

    lax.fori_loop(0, TM, body, 0, unroll=8)

    @pl.when(i == n_steps - 1)
    def _():
        wait_slot(slot)
        if n_steps > 1:
            wait_slot(1 - slot)


def _dispatch(h2, dest_tiles, pad_start, pad_len, n_used, n_tiles):
    ntok, d = h2.shape
    n_steps = ntok // TM
    grid_spec = pltpu.PrefetchScalarGridSpec(
        num_scalar_prefetch=3,
        grid=(n_steps,),
        in_specs=[pl.BlockSpec((1, 1, TOP_K * TM), lambda i, ps, pn, nu: (i, 0, 0), memory_space=pltpu.SMEM),
                  pl.BlockSpec((TM, d), lambda i, ps, pn, nu: (i, 0))],
        out_specs=pl.BlockSpec(memory_space=pl.ANY),
        scratch_shapes=[pltpu.VMEM((2, TM, d), F32), pltpu.SemaphoreType.DMA((2,)), pltpu.SemaphoreType.DMA(())],
    )
    return pl.pallas_call(
        functools.partial(_dispatch_kernel, n_steps=n_steps, n_tiles=n_tiles),
        grid_spec=grid_spec,
        out_shape=jax.ShapeDtypeStruct((n_tiles * TM, d), F32),
        compiler_params=_cparams(("arbitrary",)),
        name="moe_dispatch",
    )(pad_start, pad_len, n_used, dest_tiles, h2)


def _moe_kernel(te_ref, nu_ref, x_ref, wg_ref, wl_ref, bg_ref, bl_ref, wd_ref, bd_ref, y_ref):
    i = pl.program_id(0)

    @pl.when(i < nu_ref[0])
    def _():
        x = x_ref[...].astype(BF16)
        acts = []
        for c in range(wg_ref.shape[2] // FF_CHUNK):
            cs = slice(c * FF_CHUNK, (c + 1) * FF_CHUNK)
            g = jnp.minimum(_dot(x, wg_ref[0, :, cs]) + bg_ref[0, :, cs], SWIGLU_LIMIT)
            l = jnp.clip(_dot(x, wl_ref[0, :, cs]) + bl_ref[0, :, cs], -SWIGLU_LIMIT, SWIGLU_LIMIT)
            acts.append((g * _sigmoid(SWIGLU_ALPHA * g) * (l + 1.0)).astype(BF16))
        act = jnp.concatenate(acts, axis=-1)
        y_ref[...] = _dot(act, wd_ref[0]) + bd_ref[0]

    @pl.when(i >= nu_ref[0])
    def _():
        y_ref[...] = jnp.zeros(y_ref.shape, F32)


def _moe(xs, tile_expert, n_used, wg, wl, bg, bl, wd, bd):
    rows, d = xs.shape
    n_tiles = rows // TM
    ff = wg.shape[2]
    by_expert = lambda shape: pl.BlockSpec((1,) + shape, lambda i, te, nu: (te[i], 0, 0))
    grid_spec = pltpu.PrefetchScalarGridSpec(
        num_scalar_prefetch=2,
        grid=(n_tiles,),
        in_specs=[pl.BlockSpec((TM, d), lambda i, te, nu: (jnp.minimum(i, jnp.maximum(nu[0] - 1, 0)), 0)),
                  by_expert((d, ff)), by_expert((d, ff)), by_expert((1, ff)), by_expert((1, ff)),
                  by_expert((ff, d)), by_expert((1, d))],
        out_specs=pl.BlockSpec((TM, d), lambda i, te, nu: (i, 0)),
    )
    return pl.pallas_call(
        _moe_kernel,
        grid_spec=grid_spec,
        out_shape=jax.ShapeDtypeStruct((rows, d), F32),
        compiler_params=_cparams(("arbitrary",)),
        name="moe_experts",
    )(tile_expert, n_used, xs, wg, wl, bg, bl, wd, bd)


def _route_plan(top_e, rank, counts_f, ntok):
    n_tiles = ntok * TOP_K // TM + N_EXPERTS
    i32 = jnp.int32
    counts = counts_f[0, :N_EXPERTS].astype(i32)
    padded = ((counts + TM - 1) // TM) * TM
    gend = jnp.cumsum(padded).astype(i32)
    gstart = gend - padded
    n_used = gend[-1] // TM
    tiles = jnp.arange(n_tiles, dtype=i32)
    te_raw = jnp.sum((gend[None, :] <= (tiles * TM)[:, None]).astype(i32), axis=1)
    te_raw = jnp.minimum(te_raw, N_EXPERTS - 1)
    last = jnp.sum(jnp.where(tiles == n_used - 1, te_raw, 0))
    tile_expert = jnp.where(tiles < n_used, te_raw, last).astype(i32)
    experts = jnp.arange(N_EXPERTS, dtype=i32)
    dest = rank + jnp.sum(jnp.where(top_e[..., None] == experts, gstart, 0), axis=-1)
    dest_tiles = dest.astype(i32).reshape(ntok // TM, 1, TM * TOP_K)
    return (tile_expert, n_used.reshape(1).astype(i32), dest_tiles, (gstart + counts).astype(i32),
            (padded - counts).astype(i32), n_tiles)


def _combine_kernel(dest_ref, destn_ref, ys_hbm, x_ref, p_ref, mod_ref, fn_ref, o_ref, gbuf, sem,
                    *, n_steps, final):
    j = pl.program_id(0)
    slot = lax.rem(j, 2)

    def gather(idx_ref, sl):
        def body(r, carry):
            for kk in range(TOP_K):
                dd = idx_ref[0, 0, r * TOP_K + kk]
                pltpu.make_async_copy(ys_hbm.at[pl.ds(dd, 1), :], gbuf.at[sl, kk, pl.ds(r, 1), :], sem.at[sl]).start()
            return carry
        lax.fori_loop(0, TM, body, 0, unroll=8)

    @pl.when(j == 0)
    def _():
        gather(dest_ref, 0)

    @pl.when(j + 1 < n_steps)
    def _():
        gather(destn_ref, 1 - slot)

    for kk in range(TOP_K):
        pltpu.make_async_copy(gbuf.at[slot, kk], gbuf.at[slot, kk], sem.at[slot]).wait()

    p = p_ref[...]
    ff = p[:, 0:1] * gbuf[slot, 0]
    for kk in range(1, TOP_K):
        ff = ff + p[:, kk:kk + 1] * gbuf[slot, kk]
    x2 = x_ref[...] + mod_ref[0, 0, 5:6, :] * ff
    if final:
        x2 = x2 * lax.rsqrt(jnp.mean(x2 * x2, axis=-1, keepdims=True) + EPS) * fn_ref[...]
    o_ref[...] = x2


def _combine(x1, ys, probs, dest_tiles, modsel, final_norm, nt, nct, final):
    ntok, d = x1.shape
    b = ntok // (nt * TM)
    first = nct if final else 0
    n_out = nt - first
    n_steps = b * n_out
    tile = lambda j: (j // n_out) * nt + first + j % n_out
    nxt = lambda j: tile(jnp.minimum(j + 1, n_steps - 1))
    smem = lambda f: pl.BlockSpec((1, 1, TOP_K * TM), f, memory_space=pltpu.SMEM)
    return pl.pallas_call(
        functools.partial(_combine_kernel, n_steps=n_steps, final=final),
        grid=(n_steps,),
        in_specs=[smem(lambda j: (tile(j), 0, 0)), smem(lambda j: (nxt(j), 0, 0)),
                  pl.BlockSpec(memory_space=pl.ANY),
                  pl.BlockSpec((TM, d), lambda j: (tile(j), 0)),
                  pl.BlockSpec((TM, LANES), lambda j: (tile(j), 0)),
                  pl.BlockSpec((1, 1, N_MOD, d),
                               lambda j: (j // n_out, (first + j % n_out >= nct).astype(jnp.int32), 0, 0)),
                  pl.BlockSpec((1, d), lambda j: (0, 0))],
        out_specs=pl.BlockSpec((TM, d), lambda j: (j, 0)),
        out_shape=jax.ShapeDtypeStruct((n_steps * TM, d), F32),
        scratch_shapes=[pltpu.VMEM((2, TOP_K, TM, d), F32), pltpu.SemaphoreType.DMA((2,))],
        compiler_params=_cparams(("arbitrary",)),
        name="combine_final" if final else "combine",
    )(dest_tiles, dest_tiles, ys, x1, probs, modsel, final_norm)


def _rope_tables(ctx_len, rows):
    row = jnp.broadcast_to(jnp.arange(rows, dtype=F32)[:, None], (rows, GRID_W)).reshape(-1)
    col = jnp.broadcast_to(jnp.arange(GRID_W, dtype=F32)[None, :], (rows, GRID_W)).reshape(-1)
    inv_freq = ROPE_THETA ** (-jnp.arange(AXIS_DIM // 2, dtype=F32) * 2.0 / AXIS_DIM)
    ar = row[:, None] * inv_freq
    ac = col[:, None] * inv_freq
    ang = jnp.concatenate([ar, ar, ac, ac], axis=-1)
    cos = jnp.concatenate([jnp.ones((ctx_len, HEAD_DIM), F32), jnp.cos(ang)], axis=0)
    sin = jnp.concatenate([jnp.zeros((ctx_len, HEAD_DIM), F32), jnp.sin(ang)], axis=0)
    first_half = (jnp.arange(HEAD_DIM) % AXIS_DIM) < AXIS_DIM // 2
    return cos, jnp.where(first_half, -sin, 0.0), jnp.where(first_half, 0.0, sin)


def _dft_tables(n, width):
    k = jnp.arange(n, dtype=jnp.int32)
    ang = ((k[:, None] * k[None, :]) % n).astype(F32) * (2.0 * math.pi / n)
    scale = 1.0 / math.sqrt(n * width)
    return (jnp.cos(ang) * scale).astype(BF16), (-jnp.sin(ang) * scale).astype(BF16)


def _channel_dft_tables():
    c = jnp.arange(FOURIER_W, dtype=jnp.int32)
    same = (c[:, None] // FOURIER_GROUP_W) == (c[None, :] // FOURIER_GROUP_W)
    kk = (c[:, None] % FOURIER_GROUP_W) * (c[None, :] % FOURIER_GROUP_W) % FOURIER_GROUP_W
    ang = kk.astype(F32) * (2.0 * math.pi / FOURIER_GROUP_W)
    return (jnp.where(same, jnp.cos(ang), 0.0).astype(BF16), jnp.where(same, jnp.sin(ang), 0.0).astype(BF16))


def kernel(x, c, ctx, c_ctx, w_mod, b_mod, norm_mix, norm_ffn, w_in, q_norm, k_norm, sconv_w,
           conf_dw_w, conf_dw_b, conf_ln_g, conf_ln_b, w_out, router_w, router_b, w_up, b_up,
           w_down, b_down, final_norm):
    b, s, d = x.shape
    ctx_len = ctx.shape[1]
    depth = w_mod.shape[0]
    assert ctx_len % TM == 0 and s % TM == 0 and s % GRID_W == 0
    t = ctx_len + s
    nct = ctx_len // TM
    ntok = b * t

    cos, sin_a, sin_b = _rope_tables(ctx_len, s // GRID_W)
    ccb, scb = _channel_dft_tables()
    cctx, sctx = _dft_tables(ctx_len, FOURIER_GROUP_W)
    clat, slat = _dft_tables(s, FOURIER_GROUP_W)

    rows = ((b + 1 + 7) // 8) * 8
    cvec = jnp.zeros((rows, d), F32).at[:b].set(c).at[b].set(c_ctx)

    xa = jnp.concatenate([ctx, x], axis=1)
    out = None
    for l in range(depth):
        mod = _modulation(cvec, w_mod[l], b_mod[l]).reshape(rows, N_MOD, d)
        modsel = jnp.stack([jnp.broadcast_to(mod[b], (b, N_MOD, d)), mod[:b]], axis=1)

        q, k, v, gate, cx, fo, pg = _in_proj(
            xa, modsel, norm_mix[l].reshape(1, d), w_in[l].astype(BF16),
            q_norm[l].reshape(1, HEAD_DIM), k_norm[l].reshape(1, HEAD_DIM), cos, sin_a, sin_b, nct)
        ya = _attention(q, k, v, nct)
        yf = _fourier(fo, ccb, scb, cctx, sctx, clat, slat, nct)
        ysc, ycf = _local(gate, cx, pg, sconv_w[l], conf_dw_w[l], conf_dw_b[l].reshape(1, CONF_W),
                          conf_ln_g[l].reshape(1, CONF_W), conf_ln_b[l].reshape(1, CONF_W), nct)

        rw = jnp.zeros((d, LANES), F32).at[:, :N_EXPERTS].set(router_w[l])
        rw_hi, rw_lo = _split_bf16(rw)
        rb = jnp.zeros((1, LANES), F32).at[0, :N_EXPERTS].set(router_b[l])
        x1, h2, probs, top_e, rank, counts = _out_proj(
            ya, ysc, yf, ycf, xa, modsel, norm_ffn[l].reshape(1, d), w_out[l].astype(BF16), rw_hi, rw_lo, rb, nct)

        tile_expert, n_used, dest_tiles, pad_start, pad_len, n_tiles = _route_plan(
            top_e.reshape(ntok, LANES)[:, :TOP_K], rank.reshape(ntok, LANES)[:, :TOP_K], counts, ntok)
        xs = _dispatch(h2.reshape(ntok, d), dest_tiles, pad_start, pad_len, n_used, n_tiles)
        wg, wl = _deinterleave_up(w_up, l)
        bg = b_up[l][:, None, 0::2]
        bl = b_up[l][:, None, 1::2]
        ys = _moe(xs, tile_expert, n_used, wg, wl, bg, bl, w_down[l].astype(BF16), b_down[l][:, None, :])

        final = l == depth - 1
        res = _combine(x1.reshape(ntok, d), ys, probs.reshape(ntok, LANES), dest_tiles, modsel,
                       final_norm.reshape(1, d), t // TM, nct, final)
        if final:
            out = res.reshape(b, s, d)
        else:
            xa = res.reshape(b, t, d)
    return out
```

```python
import functools
import math

import jax
import jax.numpy as jnp
from jax import lax
from jax.experimental import pallas as pl
from jax.experimental.pallas import tpu as pltpu

F32 = jnp.float32
BF16 = jnp.bfloat16

HEAD_DIM = 128
N_Q_HEADS = 4
N_KV_HEADS = 2
REP = N_Q_HEADS // N_KV_HEADS
ATTN_W = N_Q_HEADS * HEAD_DIM
KV_W = N_KV_HEADS * HEAD_DIM
AXIS_DIM = HEAD_DIM // 2
ROPE_THETA = 10000.0
GRID_W = 64
SCONV_W = 256
SCONV_K = 3
FOURIER_W = 256
FOURIER_GROUPS = 4
FOURIER_GROUP_W = FOURIER_W // FOURIER_GROUPS
CONF_W = 256
CONF_K = 31
MIX_W = ATTN_W + SCONV_W + FOURIER_W + CONF_W
Q_END = ATTN_W
K_END = Q_END + KV_W
V_END = K_END + KV_W
N_EXPERTS = 32
TOP_K = 4
SWIGLU_LIMIT = 7.0
SWIGLU_ALPHA = 1.702
N_MOD = 6
EPS = 1e-6

LANES = 128
TM = 256
TME = 512
HALO = 16
FF_CHUNK = 256
VMEM_LIMIT = 56 * 1024 * 1024


def _cparams(sem):
    return pltpu.CompilerParams(dimension_semantics=sem, vmem_limit_bytes=VMEM_LIMIT)


def _split_bf16(a):
    hi = a.astype(BF16)
    lo = (a - hi.astype(F32)).astype(BF16)
    return hi, lo


def _dot(a, b):
    return jnp.dot(a, b, preferred_element_type=F32)


def _dot3(a_hi, a_lo, b_hi, b_lo):
    return _dot(a_hi, b_hi) + (_dot(a_lo, b_hi) + _dot(a_hi, b_lo))


def _sigmoid(z):
    return 1.0 / (1.0 + jnp.exp(-z))


def _mod_kernel(c_ref, w_ref, b_ref, o_ref):
    c = c_ref[...]
    act = c * _sigmoid(c)
    a_hi, a_lo = _split_bf16(act)
    w_hi, w_lo = _split_bf16(w_ref[...])
    o_ref[...] = _dot3(a_hi, a_lo, w_hi, w_lo) + b_ref[...]


def _modulation(cvec, w_mod, b_mod):
    r, d = cvec.shape
    n = w_mod.shape[1]
    bn = 1024
    return pl.pallas_call(
        _mod_kernel,
        grid=(n // bn,),
        in_specs=[pl.BlockSpec((r, d), lambda j: (0, 0)),
                  pl.BlockSpec((d, bn), lambda j: (0, j)),
                  pl.BlockSpec((1, bn), lambda j: (0, j))],
        out_specs=pl.BlockSpec((r, bn), lambda j: (0, j)),
        out_shape=jax.ShapeDtypeStruct((r, n), F32),
        compiler_params=_cparams(("arbitrary",)),
        name="modulation",
    )(cvec, w_mod, b_mod.reshape(1, n))


def _in_proj_kernel(x_ref, mod_ref, nw_ref, w_ref, qn_ref, kn_ref, cos_ref, sa_ref, sb_ref,
                    q_ref, k_ref, v_ref, gate_ref, cx_ref, fo_ref, pg_ref):
    x = x_ref[0]
    y = x * lax.rsqrt(jnp.mean(x * x, axis=-1, keepdims=True) + EPS) * nw_ref[...]
    shift = mod_ref[0, 0, 0:1, :]
    scale = mod_ref[0, 0, 1:2, :]
    h = (y * (1.0 + scale) + shift).astype(BF16)

    def proj(a, b):
        return _dot(h, w_ref[:, a:b])

    cos = cos_ref[...]
    sa = sa_ref[...]
    sb = sb_ref[...]

    def norm_rope(t, g):
        n = t * lax.rsqrt(jnp.mean(t * t, axis=-1, keepdims=True) + EPS) * g
        return n * cos + pltpu.roll(n, HEAD_DIM - AXIS_DIM // 2, 1) * sa + pltpu.roll(n, AXIS_DIM // 2, 1) * sb

    uq = proj(0, Q_END)
    qn = qn_ref[...]
    for hh in range(N_Q_HEADS):
        sl = slice(hh * HEAD_DIM, (hh + 1) * HEAD_DIM)
        q_ref[0, :, sl] = (norm_rope(uq[:, sl], qn) * (math.log2(math.e) * HEAD_DIM ** -0.5)).astype(BF16)
    uk = proj(Q_END, K_END)
    kn = kn_ref[...]
    for hh in range(N_KV_HEADS):
        sl = slice(hh * HEAD_DIM, (hh + 1) * HEAD_DIM)
        k_ref[0, hh] = norm_rope(uk[:, sl], kn).T.astype(BF16)
    v_ref[0] = proj(K_END, V_END).astype(BF16)
    o = V_END
    gate_ref[0] = proj(o, o + SCONV_W)
    cx_ref[0] = proj(o + SCONV_W, o + 2 * SCONV_W) * proj(o + 2 * SCONV_W, o + 3 * SCONV_W)
    o += 3 * SCONV_W
    fo_ref[0] = proj(o, o + FOURIER_W).astype(BF16)
    o += FOURIER_W
    pg_ref[0] = proj(o, o + CONF_W) * _sigmoid(proj(o + CONF_W, o + 2 * CONF_W))


def _in_proj(x, modsel, norm_w, w_in, q_norm, k_norm, cos, sin_a, sin_b, nct):
    b, t, d = x.shape
    nt = t // TM
    in_w = w_in.shape[1]
    tok = lambda w: pl.BlockSpec((1, TM, w), lambda bi, i: (bi, i, 0))
    full2 = lambda a: pl.BlockSpec(a.shape, lambda bi, i: (0, 0))
    rope = pl.BlockSpec((TM, HEAD_DIM), lambda bi, i: (i, 0))
    outs = [(ATTN_W, BF16), (KV_W, BF16), (KV_W, BF16), (SCONV_W, F32), (SCONV_W, F32),
            (FOURIER_W, BF16), (CONF_W, F32)]
    out_specs = [tok(w) for w, _ in outs]
    out_shape = [jax.ShapeDtypeStruct((b, t, w), dt) for w, dt in outs]
    out_specs[1] = pl.BlockSpec((1, N_KV_HEADS, HEAD_DIM, TM), lambda bi, i: (bi, 0, 0, i))
    out_shape[1] = jax.ShapeDtypeStruct((b, N_KV_HEADS, HEAD_DIM, t), BF16)
    return pl.pallas_call(
        _in_proj_kernel,
        grid=(b, nt),
        in_specs=[tok(d),
                  pl.BlockSpec((1, 1, N_MOD, d), lambda bi, i: (bi, (i >= nct).astype(jnp.int32), 0, 0)),
                  full2(norm_w),
                  pl.BlockSpec((d, in_w), lambda bi, i: (0, 0)),
                  full2(q_norm), full2(k_norm), rope, rope, rope],
        out_specs=out_specs,
        out_shape=out_shape,
        compiler_params=_cparams(("parallel", "arbitrary")),
        name="in_proj",
    )(x, modsel, norm_w, w_in, q_norm, k_norm, cos, sin_a, sin_b)


def _attn_kernel(q_ref, kt_ref, v_ref, o_ref, s_scr, p_scr, *, nct, ctx_len):
    i = pl.program_id(2)

    def run(nk):
        chunks = [slice(c * TM, (c + 1) * TM) for c in range(nk // TM)]
        for hh in range(REP):
            sl = slice(hh * HEAD_DIM, (hh + 1) * HEAD_DIM)
            q = q_ref[0, :, sl]
            m_lanes = None
            for ck in chunks:
                s = _dot(q, kt_ref[0, 0, :, ck])
                s_scr[hh, :, ck] = s
                for j in range(TM // LANES):
                    part = s[:, j * LANES:(j + 1) * LANES]
                    m_lanes = part if m_lanes is None else jnp.maximum(m_lanes, part)
            m = jnp.max(m_lanes, axis=-1, keepdims=True)
            l_lanes = jnp.zeros((TM, LANES), F32)
            for ck in chunks:
                p = jnp.exp2(s_scr[hh, :, ck] - m)
                for j in range(TM // LANES):
                    l_lanes = l_lanes + p[:, j * LANES:(j + 1) * LANES]
                p_scr[hh, :, ck] = p.astype(BF16)
            l = jnp.sum(l_lanes, axis=-1, keepdims=True)
            o = _dot(p_scr[hh, :, 0:nk], v_ref[0, 0:nk, :])
            o_ref[0, :, sl] = (o / l).astype(BF16)

    @pl.when(i < nct)
    def _():
        run(ctx_len)

    @pl.when(i >= nct)
    def _():
        run(kt_ref.shape[3])


def _attention(q, kt, v, nct):
    b, t, _ = q.shape
    nt = t // TM
    gw = REP * HEAD_DIM
    return pl.pallas_call(
        functools.partial(_attn_kernel, nct=nct, ctx_len=nct * TM),
        grid=(b, N_KV_HEADS, nt),
        in_specs=[pl.BlockSpec((1, TM, gw), lambda bi, g, i: (bi, i, g)),
                  pl.BlockSpec((1, 1, HEAD_DIM, t), lambda bi, g, i: (bi, g, 0, 0)),
                  pl.BlockSpec((1, t, HEAD_DIM), lambda bi, g, i: (bi, 0, g))],
        out_specs=pl.BlockSpec((1, TM, gw), lambda bi, g, i: (bi, i, g)),
        out_shape=jax.ShapeDtypeStruct((b, t, ATTN_W), BF16),
        scratch_shapes=[pltpu.VMEM((REP, TM, t), F32), pltpu.VMEM((REP, TM, t), BF16)],
        compiler_params=_cparams(("parallel", "parallel", "arbitrary")),
        name="attention",
    )(q, kt, v)


def _fourier_kernel(fo_ref, cc_ref, sc_ref, cctx_ref, sctx_ref, clat_ref, slat_ref, o_ref,
                    xc_scr, xs_scr, *, nct, ctx_len):
    i = pl.program_id(1)

    @pl.when(i == 0)
    def _():
        f = fo_ref[0]
        xc_scr[...] = _dot(f, cc_ref[...]).astype(BF16)
        xs_scr[...] = _dot(f, sc_ref[...]).astype(BF16)

    @pl.when(i < nct)
    def _():
        r0 = pl.multiple_of(i * TM, TM)
        o_ref[0] = (_dot(cctx_ref[pl.ds(r0, TM), :], xc_scr[0:ctx_len, :])
                    + _dot(sctx_ref[pl.ds(r0, TM), :], xs_scr[0:ctx_len, :])).astype(BF16)

    @pl.when(i >= nct)
    def _():
        o_ref[0] = (_dot(clat_ref[...], xc_scr[ctx_len:, :])
                    + _dot(slat_ref[...], xs_scr[ctx_len:, :])).astype(BF16)


def _fourier(fo, ccb, scb, cctx, sctx, clat, slat, nct):
    b, t, w = fo.shape
    nt = t // TM
    ctx_len = cctx.shape[0]
    s = clat.shape[0]
    lat = pl.BlockSpec((TM, s), lambda bi, i: (jnp.maximum(i - nct, 0), 0))
    full2 = lambda a: pl.BlockSpec(a.shape, lambda bi, i: (0, 0))
    return pl.pallas_call(
        functools.partial(_fourier_kernel, nct=nct, ctx_len=ctx_len),
        grid=(b, nt),
        in_specs=[pl.BlockSpec((1, t, w), lambda bi, i: (bi, 0, 0)),
                  full2(ccb), full2(scb), full2(cctx), full2(sctx), lat, lat],
        out_specs=pl.BlockSpec((1, TM, w), lambda bi, i: (bi, i, 0)),
        out_shape=jax.ShapeDtypeStruct((b, t, w), BF16),
        scratch_shapes=[pltpu.VMEM((t, w), BF16)] * 2,
        compiler_params=_cparams(("parallel", "arbitrary")),
        name="fourier",
    )(fo, ccb, scb, cctx, sctx, clat, slat)


def _local_kernel(gate_ref, cx_ref, cxp_ref, cxn_ref, pg_ref, pgp_ref, pgn_ref,
                  w3_ref, w31_ref, cb_ref, lg_ref, lb_ref, ysc_ref, ycf_ref, cx_scr, pg_scr,
                  *, nct, nt):
    i = pl.program_id(1)
    has_prev = jnp.logical_and(i != 0, i != nct)
    has_next = jnp.logical_and(i != nct - 1, i != nt - 1)
    pscale = jnp.where(has_prev, 1.0, 0.0).astype(F32)
    nscale = jnp.where(has_next, 1.0, 0.0).astype(F32)
    for src, prv, nxt, scr in ((cx_ref, cxp_ref, cxn_ref, cx_scr), (pg_ref, pgp_ref, pgn_ref, pg_scr)):
        scr[0:HALO, :] = prv[0] * pscale
        scr[HALO:HALO + TM, :] = src[0]
        scr[HALO + TM:, :] = nxt[0] * nscale

    def conv(scr, w_ref, taps):
        off = HALO - taps // 2
        acc = scr[off:off + TM, :] * w_ref[0:1, :]
        for j in range(1, taps):
            acc = acc + scr[off + j:off + j + TM, :] * w_ref[j:j + 1, :]
        return acc

    ysc_ref[0] = (gate_ref[0] * conv(cx_scr, w3_ref, SCONV_K)).astype(BF16)
    d = conv(pg_scr, w31_ref, CONF_K) + cb_ref[...]
    mu = jnp.mean(d, axis=-1, keepdims=True)
    dc = d - mu
    var = jnp.mean(dc * dc, axis=-1, keepdims=True)
    z = dc * lax.rsqrt(var + EPS) * lg_ref[...] + lb_ref[...]
    ycf_ref[0] = (z * _sigmoid(z)).astype(BF16)


def _local(gate, cx, pg, w3, w31, cb, lg, lb, nct):
    b, t, w = gate.shape
    nt = t // TM
    r = TM // HALO
    nh = t // HALO
    tok = pl.BlockSpec((1, TM, w), lambda bi, i: (bi, i, 0))
    prv = pl.BlockSpec((1, HALO, w), lambda bi, i: (bi, jnp.maximum(i * r - 1, 0), 0))
    nxt = pl.BlockSpec((1, HALO, w), lambda bi, i: (bi, jnp.minimum((i + 1) * r, nh - 1), 0))
    full2 = lambda a: pl.BlockSpec(a.shape, lambda bi, i: (0, 0))
    return pl.pallas_call(
        functools.partial(_local_kernel, nct=nct, nt=nt),
        grid=(b, nt),
        in_specs=[tok, tok, prv, nxt, tok, prv, nxt,
                  full2(w3), full2(w31), full2(cb), full2(lg), full2(lb)],
        out_specs=[tok, tok],
        out_shape=[jax.ShapeDtypeStruct((b, t, w), BF16)] * 2,
        scratch_shapes=[pltpu.VMEM((TM + 2 * HALO, w), F32)] * 2,
        compiler_params=_cparams(("parallel", "arbitrary")),
        name="local_mixers",
    )(gate, cx, cx, cx, pg, pg, pg, w3, w31, cb, lg, lb)


def _out_proj_kernel(ya_ref, ysc_ref, yf_ref, ycf_ref, x_ref, mod_ref, nw_ref, w_ref,
                     rwh_ref, rwl_ref, rb_ref, tri_ref, x1_ref, h2_ref, p_ref, e_ref, r_ref, cnt_ref,
                     base_scr):
    y = (_dot(ya_ref[0], w_ref[0:ATTN_W, :])
         + _dot(ysc_ref[0], w_ref[ATTN_W:ATTN_W + SCONV_W, :])
         + _dot(yf_ref[0], w_ref[ATTN_W + SCONV_W:ATTN_W + SCONV_W + FOURIER_W, :])
         + _dot(ycf_ref[0], w_ref[ATTN_W + SCONV_W + FOURIER_W:, :]))
    x1 = x_ref[0] + mod_ref[0, 0, 2:3, :] * y
    x1_ref[0] = x1
    n = x1 * lax.rsqrt(jnp.mean(x1 * x1, axis=-1, keepdims=True) + EPS) * nw_ref[...]
    h2 = n * (1.0 + mod_ref[0, 0, 4:5, :]) + mod_ref[0, 0, 3:4, :]
    h2_ref[0] = h2

    h_hi, h_lo = _split_bf16(h2)
    lane = lax.broadcasted_iota(jnp.int32, (TM, LANES), 1)
    logits = _dot3(h_hi, h_lo, rwh_ref[...], rwl_ref[...]) + rb_ref[...]
    logits = jnp.where(lane < N_EXPERTS, logits, -jnp.inf)
    vals, idxs = [], []
    for _ in range(TOP_K):
        m = jnp.max(logits, axis=-1, keepdims=True)
        am = jnp.min(jnp.where(logits == m, lane, LANES), axis=-1, keepdims=True)
        vals.append(m)
        idxs.append(am)
        logits = jnp.where(lane == am, -jnp.inf, logits)
    exps = [jnp.exp(v - vals[0]) for v in vals]
    denom = exps[0] + exps[1] + exps[2] + exps[3]
    @pl.when(jnp.logical_and(pl.program_id(0) == 0, pl.program_id(1) == 0))
    def _():
        base_scr[...] = jnp.zeros(base_scr.shape, F32)

    hits = [lane == idxs[kk] for kk in range(TOP_K)]
    cnt = jnp.zeros((TM, LANES), F32)
    for kk in range(TOP_K):
        cnt = cnt + jnp.where(hits[kk], 1.0, 0.0)
    before = _dot(tri_ref[...], cnt.astype(BF16)) + base_scr[0:1, :]
    base_scr[...] = base_scr[...] + jnp.sum(cnt, axis=0, keepdims=True)
    cnt_ref[...] = base_scr[...]

    p_out = jnp.zeros((TM, LANES), F32)
    e_out = jnp.zeros((TM, LANES), jnp.int32)
    r_out = jnp.zeros((TM, LANES), jnp.int32)
    for kk in range(TOP_K):
        rank = jnp.sum(jnp.where(hits[kk], before, 0.0), axis=-1, keepdims=True).astype(jnp.int32)
        p_out = jnp.where(lane == kk, exps[kk] / denom, p_out)
        e_out = jnp.where(lane == kk, idxs[kk], e_out)
        r_out = jnp.where(lane == kk, rank, r_out)
    p_ref[0] = p_out
    e_ref[0] = e_out
    r_ref[0] = r_out


def _out_proj(ya, ysc, yf, ycf, x, modsel, norm_w, w_out, rw_hi, rw_lo, rb, nct):
    b, t, d = x.shape
    nt = t // TM
    tok = lambda w: pl.BlockSpec((1, TM, w), lambda bi, i: (bi, i, 0))
    full2 = lambda a: pl.BlockSpec(a.shape, lambda bi, i: (0, 0))
    tri = (lax.broadcasted_iota(jnp.int32, (TM, TM), 1) < lax.broadcasted_iota(jnp.int32, (TM, TM), 0)).astype(BF16)
    i32 = jnp.int32
    return pl.pallas_call(
        _out_proj_kernel,
        grid=(b, nt),
        in_specs=[tok(ATTN_W), tok(SCONV_W), tok(FOURIER_W), tok(CONF_W), tok(d),
                  pl.BlockSpec((1, 1, N_MOD, d), lambda bi, i: (bi, (i >= nct).astype(jnp.int32), 0, 0)),
                  full2(norm_w), full2(w_out), full2(rw_hi), full2(rw_lo), full2(rb), full2(tri)],
        out_specs=[tok(d), tok(d), tok(LANES), tok(LANES), tok(LANES),
                   pl.BlockSpec((8, LANES), lambda bi, i: (0, 0))],
        out_shape=[jax.ShapeDtypeStruct((b, t, d), F32), jax.ShapeDtypeStruct((b, t, d), F32),
                   jax.ShapeDtypeStruct((b, t, LANES), F32), jax.ShapeDtypeStruct((b, t, LANES), i32),
                   jax.ShapeDtypeStruct((b, t, LANES), i32), jax.ShapeDtypeStruct((8, LANES), F32)],
        scratch_shapes=[pltpu.VMEM((8, LANES), F32)],
        compiler_params=_cparams(("arbitrary", "arbitrary")),
        name="out_proj_router",
    )(ya, ysc, yf, ycf, x, modsel, norm_w, w_out, rw_hi, rw_lo, rb, tri)


def _deinterleave_kernel(w_ref, pe_ref, po_ref, g_ref, l_ref):
    w = w_ref[0, 0].astype(BF16)
    g_ref[0] = _dot(w, pe_ref[...]).astype(BF16)
    l_ref[0] = _dot(w, po_ref[...]).astype(BF16)


def _deinterleave_up(w_up, layer):
    _, e, d, f2 = w_up.shape
    wb = 2 * FF_CHUNK
    rows = lax.broadcasted_iota(jnp.int32, (wb, FF_CHUNK), 0)
    cols = lax.broadcasted_iota(jnp.int32, (wb, FF_CHUNK), 1)
    pick_even = (rows == 2 * cols).astype(BF16)
    pick_odd = (rows == 2 * cols + 1).astype(BF16)
    out = jax.ShapeDtypeStruct((e, d, f2 // 2), BF16)
    return pl.pallas_call(
        _deinterleave_kernel,
        grid=(e, f2 // wb),
        in_specs=[pl.BlockSpec((1, 1, d, wb), lambda ei, j: (layer, ei, 0, j)),
                  pl.BlockSpec((wb, FF_CHUNK), lambda ei, j: (0, 0)),
                  pl.BlockSpec((wb, FF_CHUNK), lambda ei, j: (0, 0))],
        out_specs=[pl.BlockSpec((1, d, FF_CHUNK), lambda ei, j: (ei, 0, j))] * 2,
        out_shape=[out, out],
        compiler_params=_cparams(("parallel", "arbitrary")),
        name="deinterleave_up",
    )(w_up, pick_even, pick_odd)


def _dispatch_kernel(ps_ref, pn_ref, nu_ref, dest_ref, h_ref, xs_hbm, stage, sem, zsem, *, n_steps, n_tiles):
    i = pl.program_id(0)
    slot = lax.rem(i, 2)

    def wait_slot(sl):
        for _ in range(TOP_K):
            pltpu.make_async_copy(stage.at[sl], stage.at[sl], sem.at[sl]).wait()

    @pl.when(i == 0)
    def _():
        stage[0] = jnp.zeros(stage.shape[1:], F32)

        def per_expert(e, carry):
            start = ps_ref[e]
            n = pn_ref[e]

            def zero_row(r):
                return pltpu.make_async_copy(stage.at[0, pl.ds(lax.rem(r, TM), 1), :],
                                             xs_hbm.at[pl.ds(start + r, 1), :], zsem)

            def issue(r, c):
                zero_row(r).start()
                return c

            def drain(r, c):
                zero_row(r).wait()
                return c

            lax.fori_loop(0, n, issue, 0)
            lax.fori_loop(0, n, drain, 0)
            return carry

        lax.fori_loop(0, N_EXPERTS, per_expert, 0)

        def unused_tile(j, carry):
            cp = pltpu.make_async_copy(stage.at[0], xs_hbm.at[pl.ds(pl.multiple_of(j * TM, TM), TM), :], zsem)
            cp.start()
            cp.wait()
            return carry

        lax.fori_loop(nu_ref[0] * (TME // TM), n_tiles * (TME // TM), unused_tile, 0)

    @pl.when(i >= 2)
    def _():
        wait_slot(slot)

    stage[slot] = h_ref[...]

    def body(r, carry):
        for kk in range(TOP_K):
            dd = dest_ref[0, 0, r * TOP_K + kk]
            pltpu.make_async_copy(stage.at[slot, pl.ds(r, 1), :], xs_hbm.at[pl.ds(dd, 1), :],
                                  sem.at[slot]).start(priority=kk % 2)
        return carry

    lax.fori_loop(0, TM, body, 0, unroll=8)

    @pl.when(i == n_steps - 1)
    def _():
        wait_slot(slot)
        if n_steps > 1:
            wait_slot(1 - slot)


def _dispatch(h2, dest_tiles, pad_start, pad_len, n_used, n_tiles):
    ntok, d = h2.shape
    n_steps = ntok // TM
    grid_spec = pltpu.PrefetchScalarGridSpec(
        num_scalar_prefetch=3,
        grid=(n_steps,),
        in_specs=[pl.BlockSpec((1, 1, TOP_K * TM), lambda i, ps, pn, nu: (i, 0, 0), memory_space=pltpu.SMEM),
                  pl.BlockSpec((TM, d), lambda i, ps, pn, nu: (i, 0))],
        out_specs=pl.BlockSpec(memory_space=pl.ANY),
        scratch_shapes=[pltpu.VMEM((2, TM, d), F32), pltpu.SemaphoreType.DMA((2,)), pltpu.SemaphoreType.DMA(())],
    )
    return pl.pallas_call(
        functools.partial(_dispatch_kernel, n_steps=n_steps, n_tiles=n_tiles),
        grid_spec=grid_spec,
        out_shape=jax.ShapeDtypeStruct((n_tiles * TME, d), F32),
        compiler_params=_cparams(("arbitrary",)),
        name="moe_dispatch",
    )(pad_start, pad_len, n_used, dest_tiles, h2)


def _moe_kernel(te_ref, nu_ref, x_ref, wg_ref, wl_ref, bg_ref, bl_ref, wd_ref, bd_ref, y_ref):
    i = pl.program_id(0)

    @pl.when(i < nu_ref[0])
    def _():
        x = x_ref[...].astype(BF16)
        acts = []
        for c in range(wg_ref.shape[2] // FF_CHUNK):
            cs = slice(c * FF_CHUNK, (c + 1) * FF_CHUNK)
            g = jnp.minimum(_dot(x, wg_ref[0, :, cs]) + bg_ref[0, :, cs], SWIGLU_LIMIT)
            l = jnp.clip(_dot(x, wl_ref[0, :, cs]) + bl_ref[0, :, cs], -SWIGLU_LIMIT, SWIGLU_LIMIT)
            acts.append((g * _sigmoid(SWIGLU_ALPHA * g) * (l + 1.0)).astype(BF16))
        act = jnp.concatenate(acts, axis=-1)
        y_ref[...] = _dot(act, wd_ref[0]) + bd_ref[0]

    @pl.when(i >= nu_ref[0])
    def _():
        y_ref[...] = jnp.zeros(y_ref.shape, F32)


def _moe(xs, tile_expert, n_used, wg, wl, bg, bl, wd, bd):
    rows, d = xs.shape
    n_tiles = rows // TME
    ff = wg.shape[2]
    by_expert = lambda shape: pl.BlockSpec((1,) + shape, lambda i, te, nu: (te[i], 0, 0))
    grid_spec = pltpu.PrefetchScalarGridSpec(
        num_scalar_prefetch=2,
        grid=(n_tiles,),
        in_specs=[pl.BlockSpec((TME, d), lambda i, te, nu: (jnp.minimum(i, jnp.maximum(nu[0] - 1, 0)), 0)),
                  by_expert((d, ff)), by_expert((d, ff)), by_expert((1, ff)), by_expert((1, ff)),
                  by_expert((ff, d)), by_expert((1, d))],
        out_specs=pl.BlockSpec((TME, d), lambda i, te, nu: (i, 0)),
    )
    return pl.pallas_call(
        _moe_kernel,
        grid_spec=grid_spec,
        out_shape=jax.ShapeDtypeStruct((rows, d), F32),
        compiler_params=_cparams(("arbitrary",)),
        name="moe_experts",
    )(tile_expert, n_used, xs, wg, wl, bg, bl, wd, bd)


def _route_plan(top_e, rank, counts_f, ntok):
    n_tiles = ntok * TOP_K // TME + N_EXPERTS
    i32 = jnp.int32
    counts = counts_f[0, :N_EXPERTS].astype(i32)
    padded = ((counts + TME - 1) // TME) * TME
    gend = jnp.cumsum(padded).astype(i32)
    gstart = gend - padded
    n_used = gend[-1] // TME
    tiles = jnp.arange(n_tiles, dtype=i32)
    te_raw = jnp.sum((gend[None, :] <= (tiles * TME)[:, None]).astype(i32), axis=1)
    te_raw = jnp.minimum(te_raw, N_EXPERTS - 1)
    last = jnp.sum(jnp.where(tiles == n_used - 1, te_raw, 0))
    tile_expert = jnp.where(tiles < n_used, te_raw, last).astype(i32)
    experts = jnp.arange(N_EXPERTS, dtype=i32)
    dest = rank + jnp.sum(jnp.where(top_e[..., None] == experts, gstart, 0), axis=-1)
    dest_tiles = dest.astype(i32).reshape(ntok // TM, 1, TM * TOP_K)
    return (tile_expert, n_used.reshape(1).astype(i32), dest_tiles, (gstart + counts).astype(i32),
            (padded - counts).astype(i32), n_tiles)


def _combine_kernel(dest_ref, destn_ref, ys_hbm, x_ref, p_ref, mod_ref, fn_ref, o_ref, gbuf, sem,
                    *, n_steps, final):
    j = pl.program_id(0)
    slot = lax.rem(j, 2)

    def gather(idx_ref, sl):
        def body(r, carry):
            for kk in range(TOP_K):
                dd = idx_ref[0, 0, r * TOP_K + kk]
                pltpu.make_async_copy(ys_hbm.at[pl.ds(dd, 1), :], gbuf.at[sl, kk, pl.ds(r, 1), :],
                                      sem.at[sl]).start(priority=kk % 2)
            return carry
        lax.fori_loop(0, TM, body, 0, unroll=8)

    @pl.when(j == 0)
    def _():
        gather(dest_ref, 0)

    @pl.when(j + 1 < n_steps)
    def _():
        gather(destn_ref, 1 - slot)

    for kk in range(TOP_K):
        pltpu.make_async_copy(gbuf.at[slot, kk], gbuf.at[slot, kk], sem.at[slot]).wait()

    p = p_ref[...]
    ff = p[:, 0:1] * gbuf[slot, 0]
    for kk in range(1, TOP_K):
        ff = ff + p[:, kk:kk + 1] * gbuf[slot, kk]
    x2 = x_ref[...] + mod_ref[0, 0, 5:6, :] * ff
    if final:
        x2 = x2 * lax.rsqrt(jnp.mean(x2 * x2, axis=-1, keepdims=True) + EPS) * fn_ref[...]
    o_ref[...] = x2


def _combine(x1, ys, probs, dest_tiles, modsel, final_norm, nt, nct, final):
    ntok, d = x1.shape
    b = ntok // (nt * TM)
    first = nct if final else 0
    n_out = nt - first
    n_steps = b * n_out
    tile = lambda j: (j // n_out) * nt + first + j % n_out
    nxt = lambda j: tile(jnp.minimum(j + 1, n_steps - 1))
    smem = lambda f: pl.BlockSpec((1, 1, TOP_K * TM), f, memory_space=pltpu.SMEM)
    return pl.pallas_call(
        functools.partial(_combine_kernel, n_steps=n_steps, final=final),
        grid=(n_steps,),
        in_specs=[smem(lambda j: (tile(j), 0, 0)), smem(lambda j: (nxt(j), 0, 0)),
                  pl.BlockSpec(memory_space=pl.ANY),
                  pl.BlockSpec((TM, d), lambda j: (tile(j), 0)),
                  pl.BlockSpec((TM, LANES), lambda j: (tile(j), 0)),
                  pl.BlockSpec((1, 1, N_MOD, d),
                               lambda j: (j // n_out, (first + j % n_out >= nct).astype(jnp.int32), 0, 0)),
                  pl.BlockSpec((1, d), lambda j: (0, 0))],
        out_specs=pl.BlockSpec((TM, d), lambda j: (j, 0)),
        out_shape=jax.ShapeDtypeStruct((n_steps * TM, d), F32),
        scratch_shapes=[pltpu.VMEM((2, TOP_K, TM, d), F32), pltpu.SemaphoreType.DMA((2,))],
        compiler_params=_cparams(("arbitrary",)),
        name="combine_final" if final else "combine",
    )(dest_tiles, dest_tiles, ys, x1, probs, modsel, final_norm)


def _rope_tables(ctx_len, rows):
    row = jnp.broadcast_to(jnp.arange(rows, dtype=F32)[:, None], (rows, GRID_W)).reshape(-1)
    col = jnp.broadcast_to(jnp.arange(GRID_W, dtype=F32)[None, :], (rows, GRID_W)).reshape(-1)
    inv_freq = ROPE_THETA ** (-jnp.arange(AXIS_DIM // 2, dtype=F32) * 2.0 / AXIS_DIM)
    ar = row[:, None] * inv_freq
    ac = col[:, None] * inv_freq
    ang = jnp.concatenate([ar, ar, ac, ac], axis=-1)
    cos = jnp.concatenate([jnp.ones((ctx_len, HEAD_DIM), F32), jnp.cos(ang)], axis=0)
    sin = jnp.concatenate([jnp.zeros((ctx_len, HEAD_DIM), F32), jnp.sin(ang)], axis=0)
    first_half = (jnp.arange(HEAD_DIM) % AXIS_DIM) < AXIS_DIM // 2
    return cos, jnp.where(first_half, -sin, 0.0), jnp.where(first_half, 0.0, sin)


def _dft_tables(n, width):
    blk = math.gcd(n, TM)
    nn = jnp.arange(n, dtype=jnp.int32)

    def cos_sin(rows):
        ang = ((rows[:, None] * nn[None, :]) % n).astype(F32) * (2.0 * math.pi / n)
        return jnp.cos(ang), jnp.sin(ang)

    c0, s0 = cos_sin(jnp.arange(blk, dtype=jnp.int32))
    ci, si = cos_sin(jnp.arange(n // blk, dtype=jnp.int32) * blk)
    scale = 1.0 / math.sqrt(n * width)
    cos_t = (c0[None] * ci[:, None, :] - s0[None] * si[:, None, :]) * scale
    nsin_t = (s0[None] * ci[:, None, :] + c0[None] * si[:, None, :]) * -scale
    return cos_t.reshape(n, n).astype(BF16), nsin_t.reshape(n, n).astype(BF16)


def _channel_dft_tables():
    c = jnp.arange(FOURIER_W, dtype=jnp.int32)
    same = (c[:, None] // FOURIER_GROUP_W) == (c[None, :] // FOURIER_GROUP_W)
    kk = (c[:, None] % FOURIER_GROUP_W) * (c[None, :] % FOURIER_GROUP_W) % FOURIER_GROUP_W
    ang = kk.astype(F32) * (2.0 * math.pi / FOURIER_GROUP_W)
    return (jnp.where(same, jnp.cos(ang), 0.0).astype(BF16), jnp.where(same, jnp.sin(ang), 0.0).astype(BF16))


def kernel(x, c, ctx, c_ctx, w_mod, b_mod, norm_mix, norm_ffn, w_in, q_norm, k_norm, sconv_w,
           conf_dw_w, conf_dw_b, conf_ln_g, conf_ln_b, w_out, router_w, router_b, w_up, b_up,
           w_down, b_down, final_norm):
    b, s, d = x.shape
    ctx_len = ctx.shape[1]
    depth = w_mod.shape[0]
    assert ctx_len % TM == 0 and s % TM == 0 and s % GRID_W == 0
    t = ctx_len + s
    nct = ctx_len // TM
    ntok = b * t

    cos, sin_a, sin_b = _rope_tables(ctx_len, s // GRID_W)
    ccb, scb = _channel_dft_tables()
    cctx, sctx = _dft_tables(ctx_len, FOURIER_GROUP_W)
    clat, slat = _dft_tables(s, FOURIER_GROUP_W)

    rows = ((b + 1 + 7) // 8) * 8
    cvec = jnp.zeros((rows, d), F32).at[:b].set(c).at[b].set(c_ctx)

    xa = jnp.concatenate([ctx, x], axis=1)
    out = None
    for l in range(depth):
        mod = _modulation(cvec, w_mod[l], b_mod[l]).reshape(rows, N_MOD, d)
        modsel = jnp.stack([jnp.broadcast_to(mod[b], (b, N_MOD, d)), mod[:b]], axis=1)

        q, k, v, gate, cx, fo, pg = _in_proj(
            xa, modsel, norm_mix[l].reshape(1, d), w_in[l].astype(BF16),
            q_norm[l].reshape(1, HEAD_DIM), k_norm[l].reshape(1, HEAD_DIM), cos, sin_a, sin_b, nct)
        ya = _attention(q, k, v, nct)
        yf = _fourier(fo, ccb, scb, cctx, sctx, clat, slat, nct)
        ysc, ycf = _local(gate, cx, pg, sconv_w[l], conf_dw_w[l], conf_dw_b[l].reshape(1, CONF_W),
                          conf_ln_g[l].reshape(1, CONF_W), conf_ln_b[l].reshape(1, CONF_W), nct)

        rw = jnp.zeros((d, LANES), F32).at[:, :N_EXPERTS].set(router_w[l])
        rw_hi, rw_lo = _split_bf16(rw)
        rb = jnp.zeros((1, LANES), F32).at[0, :N_EXPERTS].set(router_b[l])
        x1, h2, probs, top_e, rank, counts = _out_proj(
            ya, ysc, yf, ycf, xa, modsel, norm_ffn[l].reshape(1, d), w_out[l].astype(BF16), rw_hi, rw_lo, rb, nct)

        tile_expert, n_used, dest_tiles, pad_start, pad_len, n_tiles = _route_plan(
            top_e.reshape(ntok, LANES)[:, :TOP_K], rank.reshape(ntok, LANES)[:, :TOP_K], counts, ntok)
        xs = _dispatch(h2.reshape(ntok, d), dest_tiles, pad_start, pad_len, n_used, n_tiles)
        wg, wl = _deinterleave_up(w_up, l)
        bg = b_up[l][:, None, 0::2]
        bl = b_up[l][:, None, 1::2]
        ys = _moe(xs, tile_expert, n_used, wg, wl, bg, bl, w_down[l].astype(BF16), b_down[l][:, None, :])

        final = l == depth - 1
        res = _combine(x1.reshape(ntok, d), ys, probs.reshape(ntok, LANES), dest_tiles, modsel,
                       final_norm.reshape(1, d), t // TM, nct, final)
        if final:
            out = res.reshape(b, s, d)
        else:
            xa = res.reshape(b, t, d)
    return out
```

```python
import functools
import math

import jax
import jax.numpy as jnp
from jax import lax
from jax.experimental import pallas as pl
from jax.experimental.pallas import tpu as pltpu

F32 = jnp.float32
BF16 = jnp.bfloat16

HEAD_DIM = 128
N_Q_HEADS = 4
N_KV_HEADS = 2
REP = N_Q_HEADS // N_KV_HEADS
ATTN_W = N_Q_HEADS * HEAD_DIM
KV_W = N_KV_HEADS * HEAD_DIM
AXIS_DIM = HEAD_DIM // 2
ROPE_THETA = 10000.0
GRID_W = 64
SCONV_W = 256
SCONV_K = 3
FOURIER_W = 256
FOURIER_GROUPS = 4
FOURIER_GROUP_W = FOURIER_W // FOURIER_GROUPS
CONF_W = 256
CONF_K = 31
MIX_W = ATTN_W + SCONV_W + FOURIER_W + CONF_W
Q_END = ATTN_W
K_END = Q_END + KV_W
V_END = K_END + KV_W
N_EXPERTS = 32
TOP_K = 4
SWIGLU_LIMIT = 7.0
SWIGLU_ALPHA = 1.702
N_MOD = 6
EPS = 1e-6

LANES = 128
TM = 256
TME = 512
HALO = 16
FF_CHUNK = 256
VMEM_LIMIT = 56 * 1024 * 1024


def _cparams(sem):
    return pltpu.CompilerParams(dimension_semantics=sem, vmem_limit_bytes=VMEM_LIMIT)


def _split_bf16(a):
    hi = a.astype(BF16)
    lo = (a - hi.astype(F32)).astype(BF16)
    return hi, lo


def _dot(a, b):
    return jnp.dot(a, b, preferred_element_type=F32)


def _dot3(a_hi, a_lo, b_hi, b_lo):
    return _dot(a_hi, b_hi) + (_dot(a_lo, b_hi) + _dot(a_hi, b_lo))


def _sigmoid(z):
    return 1.0 / (1.0 + jnp.exp(-z))


def _mod_kernel(c_ref, w_ref, b_ref, o_ref):
    c = c_ref[...]
    act = c * _sigmoid(c)
    a_hi, a_lo = _split_bf16(act)
    w_hi, w_lo = _split_bf16(w_ref[...])
    o_ref[...] = _dot3(a_hi, a_lo, w_hi, w_lo) + b_ref[...]


def _modulation(cvec, w_mod, b_mod):
    r, d = cvec.shape
    n = w_mod.shape[1]
    bn = 1024
    return pl.pallas_call(
        _mod_kernel,
        grid=(n // bn,),
        in_specs=[pl.BlockSpec((r, d), lambda j: (0, 0)),
                  pl.BlockSpec((d, bn), lambda j: (0, j)),
                  pl.BlockSpec((1, bn), lambda j: (0, j))],
        out_specs=pl.BlockSpec((r, bn), lambda j: (0, j)),
        out_shape=jax.ShapeDtypeStruct((r, n), F32),
        compiler_params=_cparams(("arbitrary",)),
        name="modulation",
    )(cvec, w_mod, b_mod.reshape(1, n))


def _in_proj_kernel(x_ref, mod_ref, nw_ref, w_ref, qn_ref, kn_ref, cos_ref, sa_ref, sb_ref,
                    q_ref, k_ref, v_ref, gate_ref, cx_ref, fo_ref, pg_ref):
    x = x_ref[0]
    y = x * lax.rsqrt(jnp.mean(x * x, axis=-1, keepdims=True) + EPS) * nw_ref[...]
    shift = mod_ref[0, 0, 0:1, :]
    scale = mod_ref[0, 0, 1:2, :]
    h = (y * (1.0 + scale) + shift).astype(BF16)

    def proj(a, b):
        return _dot(h, w_ref[:, a:b])

    cos = cos_ref[...]
    sa = sa_ref[...]
    sb = sb_ref[...]

    def norm_rope(t, g):
        n = t * lax.rsqrt(jnp.mean(t * t, axis=-1, keepdims=True) + EPS) * g
        return n * cos + pltpu.roll(n, HEAD_DIM - AXIS_DIM // 2, 1) * sa + pltpu.roll(n, AXIS_DIM // 2, 1) * sb

    uq = proj(0, Q_END)
    qn = qn_ref[...]
    for hh in range(N_Q_HEADS):
        sl = slice(hh * HEAD_DIM, (hh + 1) * HEAD_DIM)
        q_ref[0, :, sl] = (norm_rope(uq[:, sl], qn) * (math.log2(math.e) * HEAD_DIM ** -0.5)).astype(BF16)
    uk = proj(Q_END, K_END)
    kn = kn_ref[...]
    for hh in range(N_KV_HEADS):
        sl = slice(hh * HEAD_DIM, (hh + 1) * HEAD_DIM)
        k_ref[0, hh] = norm_rope(uk[:, sl], kn).T.astype(BF16)
    v_ref[0] = proj(K_END, V_END).astype(BF16)
    o = V_END
    gate_ref[0] = proj(o, o + SCONV_W)
    cx_ref[0] = proj(o + SCONV_W, o + 2 * SCONV_W) * proj(o + 2 * SCONV_W, o + 3 * SCONV_W)
    o += 3 * SCONV_W
    fo_ref[0] = proj(o, o + FOURIER_W).astype(BF16)
    o += FOURIER_W
    pg_ref[0] = proj(o, o + CONF_W) * _sigmoid(proj(o + CONF_W, o + 2 * CONF_W))


def _in_proj(x, modsel, norm_w, w_in, q_norm, k_norm, cos, sin_a, sin_b, nct):
    b, t, d = x.shape
    nt = t // TM
    in_w = w_in.shape[1]
    tok = lambda w: pl.BlockSpec((1, TM, w), lambda bi, i: (bi, i, 0))
    full2 = lambda a: pl.BlockSpec(a.shape, lambda bi, i: (0, 0))
    rope = pl.BlockSpec((TM, HEAD_DIM), lambda bi, i: (i, 0))
    outs = [(ATTN_W, BF16), (KV_W, BF16), (KV_W, BF16), (SCONV_W, F32), (SCONV_W, F32),
            (FOURIER_W, BF16), (CONF_W, F32)]
    out_specs = [tok(w) for w, _ in outs]
    out_shape = [jax.ShapeDtypeStruct((b, t, w), dt) for w, dt in outs]
    out_specs[1] = pl.BlockSpec((1, N_KV_HEADS, HEAD_DIM, TM), lambda bi, i: (bi, 0, 0, i))
    out_shape[1] = jax.ShapeDtypeStruct((b, N_KV_HEADS, HEAD_DIM, t), BF16)
    return pl.pallas_call(
        _in_proj_kernel,
        grid=(b, nt),
        in_specs=[tok(d),
                  pl.BlockSpec((1, 1, N_MOD, d), lambda bi, i: (bi, (i >= nct).astype(jnp.int32), 0, 0)),
                  full2(norm_w),
                  pl.BlockSpec((d, in_w), lambda bi, i: (0, 0)),
                  full2(q_norm), full2(k_norm), rope, rope, rope],
        out_specs=out_specs,
        out_shape=out_shape,
        compiler_params=_cparams(("parallel", "arbitrary")),
        name="in_proj",
    )(x, modsel, norm_w, w_in, q_norm, k_norm, cos, sin_a, sin_b)


def _attn_kernel(q_ref, kt_ref, v_ref, o_ref, s_scr, p_scr, *, nct, ctx_len):
    i = pl.program_id(2)

    def run(nk):
        chunks = [slice(c * TM, (c + 1) * TM) for c in range(nk // TM)]
        for hh in range(REP):
            sl = slice(hh * HEAD_DIM, (hh + 1) * HEAD_DIM)
            q = q_ref[0, :, sl]
            m_lanes = None
            for ck in chunks:
                s = _dot(q, kt_ref[0, 0, :, ck])
                s_scr[hh, :, ck] = s
                for j in range(TM // LANES):
                    part = s[:, j * LANES:(j + 1) * LANES]
                    m_lanes = part if m_lanes is None else jnp.maximum(m_lanes, part)
            m = jnp.max(m_lanes, axis=-1, keepdims=True)
            l_lanes = jnp.zeros((TM, LANES), F32)
            for ck in chunks:
                p = jnp.exp2(s_scr[hh, :, ck] - m)
                for j in range(TM // LANES):
                    l_lanes = l_lanes + p[:, j * LANES:(j + 1) * LANES]
                p_scr[hh, :, ck] = p.astype(BF16)
            l = jnp.sum(l_lanes, axis=-1, keepdims=True)
            o = _dot(p_scr[hh, :, 0:nk], v_ref[0, 0:nk, :])
            o_ref[0, :, sl] = (o / l).astype(BF16)

    @pl.when(i < nct)
    def _():
        run(ctx_len)

    @pl.when(i >= nct)
    def _():
        run(kt_ref.shape[3])


def _attention(q, kt, v, nct):
    b, t, _ = q.shape
    nt = t // TM
    gw = REP * HEAD_DIM
    return pl.pallas_call(
        functools.partial(_attn_kernel, nct=nct, ctx_len=nct * TM),
        grid=(b, N_KV_HEADS, nt),
        in_specs=[pl.BlockSpec((1, TM, gw), lambda bi, g, i: (bi, i, g)),
                  pl.BlockSpec((1, 1, HEAD_DIM, t), lambda bi, g, i: (bi, g, 0, 0)),
                  pl.BlockSpec((1, t, HEAD_DIM), lambda bi, g, i: (bi, 0, g))],
        out_specs=pl.BlockSpec((1, TM, gw), lambda bi, g, i: (bi, i, g)),
        out_shape=jax.ShapeDtypeStruct((b, t, ATTN_W), BF16),
        scratch_shapes=[pltpu.VMEM((REP, TM, t), F32), pltpu.VMEM((REP, TM, t), BF16)],
        compiler_params=_cparams(("parallel", "parallel", "arbitrary")),
        name="attention",
    )(q, kt, v)


def _fourier_kernel(fo_ref, cc_ref, sc_ref, cctx_ref, sctx_ref, clat_ref, slat_ref, o_ref,
                    xc_scr, xs_scr, *, nct, ctx_len):
    i = pl.program_id(1)

    @pl.when(i == 0)
    def _():
        f = fo_ref[0]
        xc_scr[...] = _dot(f, cc_ref[...]).astype(BF16)
        xs_scr[...] = _dot(f, sc_ref[...]).astype(BF16)

    @pl.when(i < nct)
    def _():
        r0 = pl.multiple_of(i * TM, TM)
        o_ref[0] = (_dot(cctx_ref[pl.ds(r0, TM), :], xc_scr[0:ctx_len, :])
                    + _dot(sctx_ref[pl.ds(r0, TM), :], xs_scr[0:ctx_len, :])).astype(BF16)

    @pl.when(i >= nct)
    def _():
        o_ref[0] = (_dot(clat_ref[...], xc_scr[ctx_len:, :])
                    + _dot(slat_ref[...], xs_scr[ctx_len:, :])).astype(BF16)


def _fourier(fo, ccb, scb, cctx, sctx, clat, slat, nct):
    b, t, w = fo.shape
    nt = t // TM
    ctx_len = cctx.shape[0]
    s = clat.shape[0]
    lat = pl.BlockSpec((TM, s), lambda bi, i: (jnp.maximum(i - nct, 0), 0))
    full2 = lambda a: pl.BlockSpec(a.shape, lambda bi, i: (0, 0))
    return pl.pallas_call(
        functools.partial(_fourier_kernel, nct=nct, ctx_len=ctx_len),
        grid=(b, nt),
        in_specs=[pl.BlockSpec((1, t, w), lambda bi, i: (bi, 0, 0)),
                  full2(ccb), full2(scb), full2(cctx), full2(sctx), lat, lat],
        out_specs=pl.BlockSpec((1, TM, w), lambda bi, i: (bi, i, 0)),
        out_shape=jax.ShapeDtypeStruct((b, t, w), BF16),
        scratch_shapes=[pltpu.VMEM((t, w), BF16)] * 2,
        compiler_params=_cparams(("parallel", "arbitrary")),
        name="fourier",
    )(fo, ccb, scb, cctx, sctx, clat, slat)


def _local_kernel(gate_ref, cx_ref, cxp_ref, cxn_ref, pg_ref, pgp_ref, pgn_ref,
                  w3_ref, w31_ref, cb_ref, lg_ref, lb_ref, ysc_ref, ycf_ref, cx_scr, pg_scr,
                  *, nct, nt):
    i = pl.program_id(1)
    has_prev = jnp.logical_and(i != 0, i != nct)
    has_next = jnp.logical_and(i != nct - 1, i != nt - 1)
    pscale = jnp.where(has_prev, 1.0, 0.0).astype(F32)
    nscale = jnp.where(has_next, 1.0, 0.0).astype(F32)
    for src, prv, nxt, scr in ((cx_ref, cxp_ref, cxn_ref, cx_scr), (pg_ref, pgp_ref, pgn_ref, pg_scr)):
        scr[0:HALO, :] = prv[0] * pscale
        scr[HALO:HALO + TM, :] = src[0]
        scr[HALO + TM:, :] = nxt[0] * nscale

    def conv(scr, w_ref, taps):
        off = HALO - taps // 2
        acc = scr[off:off + TM, :] * w_ref[0:1, :]
        for j in range(1, taps):
            acc = acc + scr[off + j:off + j + TM, :] * w_ref[j:j + 1, :]
        return acc

    ysc_ref[0] = (gate_ref[0] * conv(cx_scr, w3_ref, SCONV_K)).astype(BF16)
    d = conv(pg_scr, w31_ref, CONF_K) + cb_ref[...]
    mu = jnp.mean(d, axis=-1, keepdims=True)
    dc = d - mu
    var = jnp.mean(dc * dc, axis=-1, keepdims=True)
    z = dc * lax.rsqrt(var + EPS) * lg_ref[...] + lb_ref[...]
    ycf_ref[0] = (z * _sigmoid(z)).astype(BF16)


def _local(gate, cx, pg, w3, w31, cb, lg, lb, nct):
    b, t, w = gate.shape
    nt = t // TM
    r = TM // HALO
    nh = t // HALO
    tok = pl.BlockSpec((1, TM, w), lambda bi, i: (bi, i, 0))
    prv = pl.BlockSpec((1, HALO, w), lambda bi, i: (bi, jnp.maximum(i * r - 1, 0), 0))
    nxt = pl.BlockSpec((1, HALO, w), lambda bi, i: (bi, jnp.minimum((i + 1) * r, nh - 1), 0))
    full2 = lambda a: pl.BlockSpec(a.shape, lambda bi, i: (0, 0))
    return pl.pallas_call(
        functools.partial(_local_kernel, nct=nct, nt=nt),
        grid=(b, nt),
        in_specs=[tok, tok, prv, nxt, tok, prv, nxt,
                  full2(w3), full2(w31), full2(cb), full2(lg), full2(lb)],
        out_specs=[tok, tok],
        out_shape=[jax.ShapeDtypeStruct((b, t, w), BF16)] * 2,
        scratch_shapes=[pltpu.VMEM((TM + 2 * HALO, w), F32)] * 2,
        compiler_params=_cparams(("parallel", "arbitrary")),
        name="local_mixers",
    )(gate, cx, cx, cx, pg, pg, pg, w3, w31, cb, lg, lb)


def _out_proj_kernel(ya_ref, ysc_ref, yf_ref, ycf_ref, x_ref, mod_ref, nw_ref, w_ref,
                     rwh_ref, rwl_ref, rb_ref, tri_ref, x1_ref, h2_ref, p_ref, e_ref, r_ref, cnt_ref,
                     base_scr):
    y = (_dot(ya_ref[0], w_ref[0:ATTN_W, :])
         + _dot(ysc_ref[0], w_ref[ATTN_W:ATTN_W + SCONV_W, :])
         + _dot(yf_ref[0], w_ref[ATTN_W + SCONV_W:ATTN_W + SCONV_W + FOURIER_W, :])
         + _dot(ycf_ref[0], w_ref[ATTN_W + SCONV_W + FOURIER_W:, :]))
    x1 = x_ref[0] + mod_ref[0, 0, 2:3, :] * y
    x1_ref[0] = x1
    n = x1 * lax.rsqrt(jnp.mean(x1 * x1, axis=-1, keepdims=True) + EPS) * nw_ref[...]
    h2 = n * (1.0 + mod_ref[0, 0, 4:5, :]) + mod_ref[0, 0, 3:4, :]
    h2_ref[0] = h2

    h_hi, h_lo = _split_bf16(h2)
    lane = lax.broadcasted_iota(jnp.int32, (TM, LANES), 1)
    logits = _dot3(h_hi, h_lo, rwh_ref[...], rwl_ref[...]) + rb_ref[...]
    logits = jnp.where(lane < N_EXPERTS, logits, -jnp.inf)
    vals, idxs = [], []
    for _ in range(TOP_K):
        m = jnp.max(logits, axis=-1, keepdims=True)
        am = jnp.min(jnp.where(logits == m, lane, LANES), axis=-1, keepdims=True)
        vals.append(m)
        idxs.append(am)
        logits = jnp.where(lane == am, -jnp.inf, logits)
    exps = [jnp.exp(v - vals[0]) for v in vals]
    denom = exps[0] + exps[1] + exps[2] + exps[3]
    @pl.when(jnp.logical_and(pl.program_id(0) == 0, pl.program_id(1) == 0))
    def _():
        base_scr[...] = jnp.zeros(base_scr.shape, F32)

    hits = [lane == idxs[kk] for kk in range(TOP_K)]
    cnt = jnp.zeros((TM, LANES), F32)
    for kk in range(TOP_K):
        cnt = cnt + jnp.where(hits[kk], 1.0, 0.0)
    before = _dot(tri_ref[...], cnt.astype(BF16)) + base_scr[0:1, :]
    base_scr[...] = base_scr[...] + jnp.sum(cnt, axis=0, keepdims=True)
    cnt_ref[...] = base_scr[...]

    p_out = jnp.zeros((TM, LANES), F32)
    e_out = jnp.zeros((TM, LANES), jnp.int32)
    r_out = jnp.zeros((TM, LANES), jnp.int32)
    for kk in range(TOP_K):
        rank = jnp.sum(jnp.where(hits[kk], before, 0.0), axis=-1, keepdims=True).astype(jnp.int32)
        p_out = jnp.where(lane == kk, exps[kk] / denom, p_out)
        e_out = jnp.where(lane == kk, idxs[kk], e_out)
        r_out = jnp.where(lane == kk, rank, r_out)
    p_ref[0] = p_out
    e_ref[0] = e_out
    r_ref[0] = r_out


def _out_proj(ya, ysc, yf, ycf, x, modsel, norm_w, w_out, rw_hi, rw_lo, rb, nct):
    b, t, d = x.shape
    nt = t // TM
    tok = lambda w: pl.BlockSpec((1, TM, w), lambda bi, i: (bi, i, 0))
    full2 = lambda a: pl.BlockSpec(a.shape, lambda bi, i: (0, 0))
    tri = (lax.broadcasted_iota(jnp.int32, (TM, TM), 1) < lax.broadcasted_iota(jnp.int32, (TM, TM), 0)).astype(BF16)
    i32 = jnp.int32
    return pl.pallas_call(
        _out_proj_kernel,
        grid=(b, nt),
        in_specs=[tok(ATTN_W), tok(SCONV_W), tok(FOURIER_W), tok(CONF_W), tok(d),
                  pl.BlockSpec((1, 1, N_MOD, d), lambda bi, i: (bi, (i >= nct).astype(jnp.int32), 0, 0)),
                  full2(norm_w), full2(w_out), full2(rw_hi), full2(rw_lo), full2(rb), full2(tri)],
        out_specs=[tok(d), tok(d), tok(LANES), tok(LANES), tok(LANES),
                   pl.BlockSpec((8, LANES), lambda bi, i: (0, 0))],
        out_shape=[jax.ShapeDtypeStruct((b, t, d), F32), jax.ShapeDtypeStruct((b, t, d), F32),
                   jax.ShapeDtypeStruct((b, t, LANES), F32), jax.ShapeDtypeStruct((b, t, LANES), i32),
                   jax.ShapeDtypeStruct((b, t, LANES), i32), jax.ShapeDtypeStruct((8, LANES), F32)],
        scratch_shapes=[pltpu.VMEM((8, LANES), F32)],
        compiler_params=_cparams(("arbitrary", "arbitrary")),
        name="out_proj_router",
    )(ya, ysc, yf, ycf, x, modsel, norm_w, w_out, rw_hi, rw_lo, rb, tri)


def _deinterleave_kernel(w_ref, pe_ref, po_ref, g_ref, l_ref):
    w = w_ref[0, 0].astype(BF16)
    g_ref[0] = _dot(w, pe_ref[...]).astype(BF16)
    l_ref[0] = _dot(w, po_ref[...]).astype(BF16)


def _deinterleave_up(w_up, layer):
    _, e, d, f2 = w_up.shape
    wb = 2 * FF_CHUNK
    rows = lax.broadcasted_iota(jnp.int32, (wb, FF_CHUNK), 0)
    cols = lax.broadcasted_iota(jnp.int32, (wb, FF_CHUNK), 1)
    pick_even = (rows == 2 * cols).astype(BF16)
    pick_odd = (rows == 2 * cols + 1).astype(BF16)
    out = jax.ShapeDtypeStruct((e, d, f2 // 2), BF16)
    return pl.pallas_call(
        _deinterleave_kernel,
        grid=(e, f2 // wb),
        in_specs=[pl.BlockSpec((1, 1, d, wb), lambda ei, j: (layer, ei, 0, j)),
                  pl.BlockSpec((wb, FF_CHUNK), lambda ei, j: (0, 0)),
                  pl.BlockSpec((wb, FF_CHUNK), lambda ei, j: (0, 0))],
        out_specs=[pl.BlockSpec((1, d, FF_CHUNK), lambda ei, j: (ei, 0, j))] * 2,
        out_shape=[out, out],
        compiler_params=_cparams(("parallel", "arbitrary")),
        name="deinterleave_up",
    )(w_up, pick_even, pick_odd)


def _dispatch_kernel(ps_ref, pn_ref, nu_ref, dest_ref, h_ref, xs_hbm, stage, sem, zsem, *, n_steps, n_tiles):
    i = pl.program_id(0)
    slot = lax.rem(i, 2)

    def wait_slot(sl):
        for _ in range(TOP_K):
            pltpu.make_async_copy(stage.at[sl], stage.at[sl], sem.at[sl]).wait()

    @pl.when(i == 0)
    def _():
        stage[0] = jnp.zeros(stage.shape[1:], F32)

        def per_expert(e, carry):
            start = ps_ref[e]
            n = pn_ref[e]

            def zero_row(r):
                return pltpu.make_async_copy(stage.at[0, pl.ds(lax.rem(r, TM), 1), :],
                                             xs_hbm.at[pl.ds(start + r, 1), :], zsem)

            def issue(r, c):
                zero_row(r).start()
                return c

            def drain(r, c):
                zero_row(r).wait()
                return c

            lax.fori_loop(0, n, issue, 0)
            lax.fori_loop(0, n, drain, 0)
            return carry

        lax.fori_loop(0, N_EXPERTS, per_expert, 0)

        def unused_tile(j, carry):
            cp = pltpu.make_async_copy(stage.at[0], xs_hbm.at[pl.ds(pl.multiple_of(j * TM, TM), TM), :], zsem)
            cp.start()
            cp.wait()
            return carry

        lax.fori_loop(nu_ref[0] * (TME // TM), n_tiles * (TME // TM), unused_tile, 0)

    @pl.when(i >= 2)
    def _():
        wait_slot(slot)

    stage[slot] = h_ref[...]

    def body(r, carry):
        for kk in range(TOP_K):
            dd = dest_ref[0, 0, r * TOP_K + kk]
            pltpu.make_async_copy(stage.at[slot, pl.ds(r, 1), :], xs_hbm.at[pl.ds(dd, 1), :], sem.at[slot]).start()
        return carry

    lax.fori_loop(0, TM, body, 0, unroll=8)

    @pl.when(i == n_steps - 1)
    def _():
        wait_slot(slot)
        if n_steps > 1:
            wait_slot(1 - slot)


def _dispatch(h2, dest_tiles, pad_start, pad_len, n_used, n_tiles):
    ntok, d = h2.shape
    n_steps = ntok // TM
    grid_spec = pltpu.PrefetchScalarGridSpec(
        num_scalar_prefetch=3,
        grid=(n_steps,),
        in_specs=[pl.BlockSpec((1, 1, TOP_K * TM), lambda i, ps, pn, nu: (i, 0, 0), memory_space=pltpu.SMEM),
                  pl.BlockSpec((TM, d), lambda i, ps, pn, nu: (i, 0))],
        out_specs=pl.BlockSpec(memory_space=pl.ANY),
        scratch_shapes=[pltpu.VMEM((2, TM, d), F32), pltpu.SemaphoreType.DMA((2,)), pltpu.SemaphoreType.DMA(())],
    )
    return pl.pallas_call(
        functools.partial(_dispatch_kernel, n_steps=n_steps, n_tiles=n_tiles),
        grid_spec=grid_spec,
        out_shape=jax.ShapeDtypeStruct((n_tiles * TME, d), F32),
        compiler_params=_cparams(("arbitrary",)),
        name="moe_dispatch",
    )(pad_start, pad_len, n_used, dest_tiles, h2)


def _moe_kernel(te_ref, nu_ref, x_ref, wg_ref, wl_ref, bg_ref, bl_ref, wd_ref, bd_ref, y_ref):
    i = pl.program_id(0)

    @pl.when(i < nu_ref[0])
    def _():
        x = x_ref[...].astype(BF16)
        acts = []
        for c in range(wg_ref.shape[2] // FF_CHUNK):
            cs = slice(c * FF_CHUNK, (c + 1) * FF_CHUNK)
            g = jnp.minimum(_dot(x, wg_ref[0, :, cs]) + bg_ref[0, :, cs], SWIGLU_LIMIT)
            l = jnp.clip(_dot(x, wl_ref[0, :, cs]) + bl_ref[0, :, cs], -SWIGLU_LIMIT, SWIGLU_LIMIT)
            acts.append((g * _sigmoid(SWIGLU_ALPHA * g) * (l + 1.0)).astype(BF16))
        act = jnp.concatenate(acts, axis=-1)
        y_ref[...] = _dot(act, wd_ref[0]) + bd_ref[0]

    @pl.when(i >= nu_ref[0])
    def _():
        y_ref[...] = jnp.zeros(y_ref.shape, F32)


def _moe(xs, tile_expert, n_used, wg, wl, bg, bl, wd, bd):
    rows, d = xs.shape
    n_tiles = rows // TME
    ff = wg.shape[2]
    by_expert = lambda shape: pl.BlockSpec((1,) + shape, lambda i, te, nu: (te[i], 0, 0))
    grid_spec = pltpu.PrefetchScalarGridSpec(
        num_scalar_prefetch=2,
        grid=(n_tiles,),
        in_specs=[pl.BlockSpec((TME, d), lambda i, te, nu: (jnp.minimum(i, jnp.maximum(nu[0] - 1, 0)), 0)),
                  by_expert((d, ff)), by_expert((d, ff)), by_expert((1, ff)), by_expert((1, ff)),
                  by_expert((ff, d)), by_expert((1, d))],
        out_specs=pl.BlockSpec((TME, d), lambda i, te, nu: (i, 0)),
    )
    return pl.pallas_call(
        _moe_kernel,
        grid_spec=grid_spec,
        out_shape=jax.ShapeDtypeStruct((rows, d), F32),
        compiler_params=_cparams(("arbitrary",)),
        name="moe_experts",
    )(tile_expert, n_used, xs, wg, wl, bg, bl, wd, bd)


def _route_plan(top_e, rank, counts_f, ntok):
    n_tiles = ntok * TOP_K // TME + N_EXPERTS
    i32 = jnp.int32
    counts = counts_f[0, :N_EXPERTS].astype(i32)
    padded = ((counts + TME - 1) // TME) * TME
    gend = jnp.cumsum(padded).astype(i32)
    gstart = gend - padded
    n_used = gend[-1] // TME
    tiles = jnp.arange(n_tiles, dtype=i32)
    te_raw = jnp.sum((gend[None, :] <= (tiles * TME)[:, None]).astype(i32), axis=1)
    te_raw = jnp.minimum(te_raw, N_EXPERTS - 1)
    last = jnp.sum(jnp.where(tiles == n_used - 1, te_raw, 0))
    tile_expert = jnp.where(tiles < n_used, te_raw, last).astype(i32)
    experts = jnp.arange(N_EXPERTS, dtype=i32)
    dest = rank + jnp.sum(jnp.where(top_e[..., None] == experts, gstart, 0), axis=-1)
    dest_tiles = dest.astype(i32).reshape(ntok // TM, 1, TM * TOP_K)
    return (tile_expert, n_used.reshape(1).astype(i32), dest_tiles, (gstart + counts).astype(i32),
            (padded - counts).astype(i32), n_tiles)


def _combine_kernel(dest_ref, destn_ref, ys_hbm, x_ref, p_ref, mod_ref, fn_ref, o_ref, gbuf, sem,
                    *, n_steps, final):
    j = pl.program_id(0)
    slot = lax.rem(j, 2)

    def gather(idx_ref, sl):
        def body(r, carry):
            for kk in range(TOP_K):
                dd = idx_ref[0, 0, r * TOP_K + kk]
                pltpu.make_async_copy(ys_hbm.at[pl.ds(dd, 1), :], gbuf.at[sl, kk, pl.ds(r, 1), :], sem.at[sl]).start()
            return carry
        lax.fori_loop(0, TM, body, 0, unroll=8)

    @pl.when(j == 0)
    def _():
        gather(dest_ref, 0)

    @pl.when(j + 1 < n_steps)
    def _():
        gather(destn_ref, 1 - slot)

    for kk in range(TOP_K):
        pltpu.make_async_copy(gbuf.at[slot, kk], gbuf.at[slot, kk], sem.at[slot]).wait()

    p = p_ref[...]
    ff = p[:, 0:1] * gbuf[slot, 0]
    for kk in range(1, TOP_K):
        ff = ff + p[:, kk:kk + 1] * gbuf[slot, kk]
    x2 = x_ref[...] + mod_ref[0, 0, 5:6, :] * ff
    if final:
        x2 = x2 * lax.rsqrt(jnp.mean(x2 * x2, axis=-1, keepdims=True) + EPS) * fn_ref[...]
    o_ref[...] = x2


def _combine(x1, ys, probs, dest_tiles, modsel, final_norm, nt, nct, final):
    ntok, d = x1.shape
    b = ntok // (nt * TM)
    first = nct if final else 0
    n_out = nt - first
    n_steps = b * n_out
    tile = lambda j: (j // n_out) * nt + first + j % n_out
    nxt = lambda j: tile(jnp.minimum(j + 1, n_steps - 1))
    smem = lambda f: pl.BlockSpec((1, 1, TOP_K * TM), f, memory_space=pltpu.SMEM)
    return pl.pallas_call(
        functools.partial(_combine_kernel, n_steps=n_steps, final=final),
        grid=(n_steps,),
        in_specs=[smem(lambda j: (tile(j), 0, 0)), smem(lambda j: (nxt(j), 0, 0)),
                  pl.BlockSpec(memory_space=pl.ANY),
                  pl.BlockSpec((TM, d), lambda j: (tile(j), 0)),
                  pl.BlockSpec((TM, LANES), lambda j: (tile(j), 0)),
                  pl.BlockSpec((1, 1, N_MOD, d),
                               lambda j: (j // n_out, (first + j % n_out >= nct).astype(jnp.int32), 0, 0)),
                  pl.BlockSpec((1, d), lambda j: (0, 0))],
        out_specs=pl.BlockSpec((TM, d), lambda j: (j, 0)),
        out_shape=jax.ShapeDtypeStruct((n_steps * TM, d), F32),
        scratch_shapes=[pltpu.VMEM((2, TOP_K, TM, d), F32), pltpu.SemaphoreType.DMA((2,))],
        compiler_params=_cparams(("arbitrary",)),
        name="combine_final" if final else "combine",
    )(dest_tiles, dest_tiles, ys, x1, probs, modsel, final_norm)


def _rope_tables(ctx_len, rows):
    row = jnp.broadcast_to(jnp.arange(rows, dtype=F32)[:, None], (rows, GRID_W)).reshape(-1)
    col = jnp.broadcast_to(jnp.arange(GRID_W, dtype=F32)[None, :], (rows, GRID_W)).reshape(-1)
    inv_freq = ROPE_THETA ** (-jnp.arange(AXIS_DIM // 2, dtype=F32) * 2.0 / AXIS_DIM)
    ar = row[:, None] * inv_freq
    ac = col[:, None] * inv_freq
    ang = jnp.concatenate([ar, ar, ac, ac], axis=-1)
    cos = jnp.concatenate([jnp.ones((ctx_len, HEAD_DIM), F32), jnp.cos(ang)], axis=0)
    sin = jnp.concatenate([jnp.zeros((ctx_len, HEAD_DIM), F32), jnp.sin(ang)], axis=0)
    first_half = (jnp.arange(HEAD_DIM) % AXIS_DIM) < AXIS_DIM // 2
    return cos, jnp.where(first_half, -sin, 0.0), jnp.where(first_half, 0.0, sin)


def _dft_tables(n, width):
    blk = math.gcd(n, TM)
    nn = jnp.arange(n, dtype=jnp.int32)

    def cos_sin(rows):
        ang = ((rows[:, None] * nn[None, :]) % n).astype(F32) * (2.0 * math.pi / n)
        return jnp.cos(ang), jnp.sin(ang)

    c0, s0 = cos_sin(jnp.arange(blk, dtype=jnp.int32))
    ci, si = cos_sin(jnp.arange(n // blk, dtype=jnp.int32) * blk)
    scale = 1.0 / math.sqrt(n * width)
    cos_t = (c0[None] * ci[:, None, :] - s0[None] * si[:, None, :]) * scale
    nsin_t = (s0[None] * ci[:, None, :] + c0[None] * si[:, None, :]) * -scale
    return cos_t.reshape(n, n).astype(BF16), nsin_t.reshape(n, n).astype(BF16)


def _channel_dft_tables():
    c = jnp.arange(FOURIER_W, dtype=jnp.int32)
    same = (c[:, None] // FOURIER_GROUP_W) == (c[None, :] // FOURIER_GROUP_W)
    kk = (c[:, None] % FOURIER_GROUP_W) * (c[None, :] % FOURIER_GROUP_W) % FOURIER_GROUP_W
    ang = kk.astype(F32) * (2.0 * math.pi / FOURIER_GROUP_W)
    return (jnp.where(same, jnp.cos(ang), 0.0).astype(BF16), jnp.where(same, jnp.sin(ang), 0.0).astype(BF16))


def kernel(x, c, ctx, c_ctx, w_mod, b_mod, norm_mix, norm_ffn, w_in, q_norm, k_norm, sconv_w,
           conf_dw_w, conf_dw_b, conf_ln_g, conf_ln_b, w_out, router_w, router_b, w_up, b_up,
           w_down, b_down, final_norm):
    b, s, d = x.shape
    ctx_len = ctx.shape[1]
    depth = w_mod.shape[0]
    assert ctx_len % TM == 0 and s % TM == 0 and s % GRID_W == 0
    t = ctx_len + s
    nct = ctx_len // TM
    ntok = b * t

    cos, sin_a, sin_b = _rope_tables(ctx_len, s // GRID_W)
    ccb, scb = _channel_dft_tables()
    cctx, sctx = _dft_tables(ctx_len, FOURIER_GROUP_W)
    clat, slat = _dft_tables(s, FOURIER_GROUP_W)

    rows = ((b + 1 + 7) // 8) * 8
    cvec = jnp.zeros((rows, d), F32).at[:b].set(c).at[b].set(c_ctx)

    xa = jnp.concatenate([ctx, x], axis=1)
    out = None
    for l in range(depth):
        mod = _modulation(cvec, w_mod[l], b_mod[l]).reshape(rows, N_MOD, d)
        modsel = jnp.stack([jnp.broadcast_to(mod[b], (b, N_MOD, d)), mod[:b]], axis=1)

        q, k, v, gate, cx, fo, pg = _in_proj(
            xa, modsel, norm_mix[l].reshape(1, d), w_in[l].astype(BF16),
            q_norm[l].reshape(1, HEAD_DIM), k_norm[l].reshape(1, HEAD_DIM), cos, sin_a, sin_b, nct)
        ya = _attention(q, k, v, nct)
        yf = _fourier(fo, ccb, scb, cctx, sctx, clat, slat, nct)
        ysc, ycf = _local(gate, cx, pg, sconv_w[l], conf_dw_w[l], conf_dw_b[l].reshape(1, CONF_W),
                          conf_ln_g[l].reshape(1, CONF_W), conf_ln_b[l].reshape(1, CONF_W), nct)

        rw = jnp.zeros((d, LANES), F32).at[:, :N_EXPERTS].set(router_w[l])
        rw_hi, rw_lo = _split_bf16(rw)
        rb = jnp.zeros((1, LANES), F32).at[0, :N_EXPERTS].set(router_b[l])
        x1, h2, probs, top_e, rank, counts = _out_proj(
            ya, ysc, yf, ycf, xa, modsel, norm_ffn[l].reshape(1, d), w_out[l].astype(BF16), rw_hi, rw_lo, rb, nct)

        tile_expert, n_used, dest_tiles, pad_start, pad_len, n_tiles = _route_plan(
            top_e.reshape(ntok, LANES)[:, :TOP_K], rank.reshape(ntok, LANES)[:, :TOP_K], counts, ntok)
        xs = _dispatch(h2.reshape(ntok, d), dest_tiles, pad_start, pad_len, n_used, n_tiles)
        wg, wl = _deinterleave_up(w_up, l)
        bg = b_up[l][:, None, 0::2]
        bl = b_up[l][:, None, 1::2]
        ys = _moe(xs, tile_expert, n_used, wg, wl, bg, bl, w_down[l].astype(BF16), b_down[l][:, None, :])

        final = l == depth - 1
        res = _combine(x1.reshape(ntok, d), ys, probs.reshape(ntok, LANES), dest_tiles, modsel,
                       final_norm.reshape(1, d), t // TM, nct, final)
        if final:
            out = res.reshape(b, s, d)
        else:
            xa = res.reshape(b, t, d)
    return out
```
